```python
import jax
import jax.numpy as jnp
from jax import lax
import numpy as np

D_MODEL = 2048
BATCH = 1
SEQ = 8192
DEPTH = 4

GRID_W = 64
CTX_LEN = 256
EPS = 1e-6
D_FF = 5632
N_MOD = 9
D_GROUP = D_MODEL // 4
D_MIX = 4 * D_GROUP
LRU_W = D_GROUP
LRU_HEADS = 8
LRU_HD = LRU_W // LRU_HEADS
LRU_CONV = 4
LRU_C = 8.0
ATT_HD = 128
ATT_QH = D_GROUP // ATT_HD
ATT_KVH = 2
ATT_G = ATT_QH // ATT_KVH
ATT_BLOCK = 128
ROPE_THETA = 10000.0
GLA_HEADS = 4
GLA_DK = 64
GLA_DV = D_GROUP // GLA_HEADS
GLA_RANK = 16
GLA_TAU = 16.0
GLA_CHUNK = 64
FNET_GROUPS = 4
FNET_W = D_GROUP

SPLIT_SIZES = (LRU_W, LRU_W,
               ATT_QH * ATT_HD, ATT_KVH * ATT_HD, ATT_KVH * ATT_HD,
               GLA_HEADS * GLA_DK, GLA_HEADS * GLA_DK, GLA_HEADS * GLA_DV, GLA_HEADS * GLA_DV, 2 * GLA_RANK,
               FNET_W)
D_IN = sum(SPLIT_SIZES)
SPLIT_POINTS = [sum(SPLIT_SIZES[:i + 1]) for i in range(len(SPLIT_SIZES) - 1)]

kernel_name = 'hybrid_parallel_groups_dit_prefix'


def rmsnorm(x, g):
    xf = x.astype(jnp.float32)
    y = xf * lax.rsqrt(jnp.mean(jnp.square(xf), axis=-1, keepdims=True) + EPS)
    return (y * g.astype(jnp.float32)).astype(x.dtype)


def head_rms(t, g):
    return t * lax.rsqrt(jnp.mean(t * t, axis=-1, keepdims=True) + EPS) * g.astype(jnp.float32)


def modulate(h, shift, scale):
    return h * (1 + scale) + shift


def swiglu(h, w13, w2):
    gate, up = jnp.split(h @ w13, 2, axis=-1)
    return (jax.nn.silu(gate) * up) @ w2


def _flip(t):
    return jnp.flip(t, axis=1)


def _ident(t):
    return t


def dwconv_centred(u, w, b):
    T = u.shape[1]
    lo = LRU_CONV // 2
    up = jnp.pad(u, ((0, 0), (lo, LRU_CONV - 1 - lo), (0, 0)))
    return b + sum(w[j] * up[:, j:j + T] for j in range(LRU_CONV))


def _lin_comb(l, r):
    a_l, b_l = l
    a_r, b_r = r
    return a_l * a_r, a_r * b_l + b_r


def scan_forward(a, b, h0):
    b = b.at[:, 0].add(a[:, 0] * h0)
    _, h = lax.associative_scan(_lin_comb, (a, b), axis=1)
    return h


def rglru_coeffs(u, w_gate, b_gate, lam):
    B, T, _ = u.shape
    uh = u.reshape(B, T, LRU_HEADS, LRU_HD)
    gates = jnp.einsum('bthi,ghij->gbthj', uh, w_gate.astype(jnp.float32)).reshape(2, B, T, LRU_W)
    gates = gates + b_gate.astype(jnp.float32)[:, None, None, :]
    r = jax.nn.sigmoid(gates[0])
    i = jax.nn.sigmoid(gates[1])
    log_a = -LRU_C * r * jax.nn.softplus(-lam.astype(jnp.float32))
    a = jnp.exp(log_a)
    return a, jnp.sqrt(-jnp.expm1(2.0 * log_a)) * (i * u)


def lru_mixer(xc, gc, xl, gl, conv_w, conv_b, w_gate, b_gate, lam, need_ctx):
    f32 = jnp.float32
    uc = dwconv_centred(xc, conv_w, conv_b).astype(f32)
    ul = dwconv_centred(xl, conv_w, conv_b).astype(f32)
    hc_dirs, hl_dirs = [], []
    for d in range(2):
        fl = _flip if d else _ident
        ac, bc = rglru_coeffs(fl(uc), w_gate[d], b_gate[d], lam[d])
        hc = scan_forward(ac, bc, jnp.zeros_like(uc[:, 0]))
        al, bl = rglru_coeffs(fl(ul), w_gate[d], b_gate[d], lam[d])
        hl = scan_forward(al, bl, hc[:, -1])
        hc_dirs.append(fl(hc))
        hl_dirs.append(fl(hl))
    yl = ((hl_dirs[0] + hl_dirs[1]) * jax.nn.gelu(gl.astype(f32))).astype(xl.dtype)
    yc = ((hc_dirs[0] + hc_dirs[1]) * jax.nn.gelu(gc.astype(f32))).astype(xc.dtype) if need_ctx else None
    return yl, yc


def rope_axis(t, pos):
    n = t.shape[-1]
    half = n // 2
    freqs = ROPE_THETA ** (-jnp.arange(half, dtype=jnp.float32) / half)
    ang = pos.astype(jnp.float32)[:, None] * freqs
    cos = jnp.cos(ang)[:, None, :]
    sin = jnp.sin(ang)[:, None, :]
    t1, t2 = t[..., :half], t[..., half:]
    return jnp.concatenate([t1 * cos - t2 * sin, t1 * sin + t2 * cos], axis=-1)


def rope_2d(t, rows, cols):
    half = t.shape[-1] // 2
    return jnp.concatenate([rope_axis(t[..., :half], rows), rope_axis(t[..., half:], cols)], axis=-1)


def gqa_attend(q, k, v):
    s = jnp.einsum('bqkgd,bskd->bkgqs', q, k) * (ATT_HD ** -0.5)
    p = jax.nn.softmax(s, axis=-1)
    return jnp.einsum('bkgqs,bskd->bqkgd', p, v)


def attn_mixer(qc, kc, vc, ql, kl, vl, qk_g, rows, cols, need_ctx):
    f32 = jnp.float32
    B, S, _ = ql.shape
    Tc = qc.shape[1]

    def heads(t, h):
        return t.astype(f32).reshape(t.shape[0], t.shape[1], h, ATT_HD)

    kc_ = head_rms(heads(kc, ATT_KVH), qk_g[1])
    vc_ = heads(vc, ATT_KVH)
    ql_ = rope_2d(head_rms(heads(ql, ATT_QH), qk_g[0]), rows, cols)
    kl_ = rope_2d(head_rms(heads(kl, ATT_KVH), qk_g[1]), rows, cols)
    k_all = jnp.concatenate([kl_, kc_], axis=1)
    v_all = jnp.concatenate([heads(vl, ATT_KVH), vc_], axis=1)
    nb = S // ATT_BLOCK
    qb = ql_.reshape(B, nb, ATT_BLOCK, ATT_KVH, ATT_G, ATT_HD).swapaxes(0, 1)
    ol = lax.map(lambda qblk: gqa_attend(qblk, k_all, v_all), qb)
    yl = ol.swapaxes(0, 1).reshape(B, S, ATT_QH * ATT_HD).astype(ql.dtype)
    yc = None
    if need_ctx:
        qc_ = head_rms(heads(qc, ATT_QH), qk_g[0]).reshape(B, Tc, ATT_KVH, ATT_G, ATT_HD)
        yc = gqa_attend(qc_, kc_, vc_).reshape(B, Tc, ATT_QH * ATT_HD).astype(qc.dtype)
    return yl, yc


def gla_chunked(q, k, v, log_a, s0):
    B, T, H, DK = q.shape
    C = GLA_CHUNK
    N = T // C

    def rs(t):
        return t.reshape(B, N, C, H, t.shape[-1])

    q, k, v, log_a = rs(q), rs(k), rs(v), rs(log_a)
    bcum = jnp.cumsum(log_a, axis=2)
    blast = bcum[:, :, -1:]
    q_t = q * jnp.exp(bcum)
    k_in = k * jnp.exp(-bcum)
    k_out = k * jnp.exp(blast - bcum)
    mask = jnp.tril(jnp.ones((C, C), dtype=bool))
    att = jnp.where(mask, jnp.einsum('bnthd,bnshd->bnhts', q_t, k_in), 0.0)
    o_intra = jnp.einsum('bnhts,bnshv->bnthv', att, v)
    d_state = jnp.einsum('bnshd,bnshv->bnhdv', k_out, v)
    decay = jnp.exp(blast[:, :, 0])

    def step(s, inp):
        dec, ds = inp
        return dec[..., None] * s + ds, s

    s_final, s_prev = lax.scan(step, s0, (decay.swapaxes(0, 1), d_state.swapaxes(0, 1)))
    o_inter = jnp.einsum('bnthd,bnhdv->bnthv', q_t, s_prev.swapaxes(0, 1))
    return (o_intra + o_inter).reshape(B, T, H, -1), s_final


def gla_prep(q, k, v, g, lr, gate_w, gate_b):
    f32 = jnp.float32
    B, T, _ = q.shape

    def hk(t):
        return t.astype(f32).reshape(B, T, GLA_HEADS, -1)

    log_a = [hk(jax.nn.log_sigmoid(lr[..., d * GLA_RANK:(d + 1) * GLA_RANK].astype(f32) @ gate_w[d].astype(f32)
                                   + gate_b[d].astype(f32)) / GLA_TAU) for d in range(2)]
    return hk(q) * (GLA_DK ** -0.5), hk(k), hk(v), log_a, hk(g)


def gla_out(o, g, out_g, dtype):
    B, T = o.shape[0], o.shape[1]
    return (head_rms(o, out_g) * jax.nn.silu(g)).reshape(B, T, GLA_HEADS * GLA_DV).astype(dtype)


def gla_mixer(ctx_parts, lat_parts, gate_w, gate_b, out_g, need_ctx):
    qc, kc, vc, lac, gc = gla_prep(*ctx_parts, gate_w, gate_b)
    ql, kl, vl, lal, gl = gla_prep(*lat_parts, gate_w, gate_b)
    s0 = jnp.zeros((ql.shape[0], GLA_HEADS, GLA_DK, GLA_DV), jnp.float32)
    oc_dirs, ol_dirs = [], []
    for d in range(2):
        fl = _flip if d else _ident
        oc, sc = gla_chunked(fl(qc), fl(kc), fl(vc), fl(lac[d]), s0)
        ol, _ = gla_chunked(fl(ql), fl(kl), fl(vl), fl(lal[d]), sc)
        oc_dirs.append(fl(oc))
        ol_dirs.append(fl(ol))
    yl = gla_out(ol_dirs[0] + ol_dirs[1], gl, out_g, lat_parts[0].dtype)
    yc = gla_out(oc_dirs[0] + oc_dirs[1], gc, out_g, ctx_parts[0].dtype) if need_ctx else None
    return yl, yc


def fourier_mixer(u):
    B, T, _ = u.shape
    uf = u.astype(jnp.float32).reshape(B, T, FNET_GROUPS, FNET_W // FNET_GROUPS)
    y = jnp.fft.fft2(uf, axes=(1, 3), norm='ortho').real
    return y.reshape(B, T, FNET_W).astype(u.dtype)


def trunk_layer(xl, xc, mod_l, mod_c, norm_g, ffn_w13, ffn_w2, w_in, w_out,
                lru_conv_w, lru_conv_b, lru_gate_w, lru_gate_b, lru_lambda,
                qk_g, gla_gate_w, gla_gate_b, gla_out_g, rows, cols, need_ctx):
    def sub_in(x, m, j):
        return modulate(rmsnorm(x, norm_g[j]), m[..., 3 * j, :], m[..., 3 * j + 1, :])

    xl = xl + 0.5 * mod_l[..., 2, :] * swiglu(sub_in(xl, mod_l, 0), ffn_w13[0], ffn_w2[0])
    xc = xc + 0.5 * mod_c[..., 2, :] * swiglu(sub_in(xc, mod_c, 0), ffn_w13[0], ffn_w2[0])
    pl = jnp.split(sub_in(xl, mod_l, 1) @ w_in, SPLIT_POINTS, axis=-1)
    pc = jnp.split(sub_in(xc, mod_c, 1) @ w_in, SPLIT_POINTS, axis=-1)
    ya_l, ya_c = lru_mixer(pc[0], pc[1], pl[0], pl[1], lru_conv_w, lru_conv_b,
                           lru_gate_w, lru_gate_b, lru_lambda, need_ctx)
    yb_l, yb_c = attn_mixer(pc[2], pc[3], pc[4], pl[2], pl[3], pl[4], qk_g, rows, cols, need_ctx)
    yc_l, yc_c = gla_mixer(pc[5:10], pl[5:10], gla_gate_w, gla_gate_b, gla_out_g, need_ctx)
    yd_l = fourier_mixer(pl[10])
    mix_l = jnp.concatenate([ya_l, yb_l, yc_l, yd_l], axis=-1) @ w_out
    xl = xl + mod_l[..., 5, :] * mix_l
    xl = xl + 0.5 * mod_l[..., 8, :] * swiglu(sub_in(xl, mod_l, 2), ffn_w13[1], ffn_w2[1])
    if need_ctx:
        yd_c = fourier_mixer(pc[10])
        mix_c = jnp.concatenate([ya_c, yb_c, yc_c, yd_c], axis=-1) @ w_out
        xc = xc + mod_c[..., 5, :] * mix_c
        xc = xc + 0.5 * mod_c[..., 8, :] * swiglu(sub_in(xc, mod_c, 2), ffn_w13[1], ffn_w2[1])
    return xl, xc


def setup_inputs(seed: int = 0) -> dict:
    key = jax.random.key(seed)
    ks = jax.random.split(key, 21)
    f32 = jnp.float32

    def nrm(k, shape, scale):
        return jax.random.normal(k, shape, f32) * scale

    L, D = DEPTH, D_MODEL
    a0 = jax.random.uniform(ks[16], (L, 2, LRU_W), f32, 0.9, 0.999)
    p0 = a0 ** (1.0 / LRU_C)
    return {
        'x': nrm(ks[0], (BATCH, SEQ, D), 1.0),
        'c': nrm(ks[1], (BATCH, D), 1.0),
        'ctx': nrm(ks[2], (BATCH, CTX_LEN, D), 1.0),
        'c_ctx': nrm(ks[3], (D,), 1.0),
        'mod_w': nrm(ks[4], (L, D, N_MOD * D), 0.5 * D ** -0.5),
        'mod_b': nrm(ks[5], (L, N_MOD * D), 0.02),
        'norm_g': 1.0 + nrm(ks[6], (L, 3, D), 0.05),
        'final_g': 1.0 + nrm(ks[7], (D,), 0.05),
        'ffn_w13': nrm(ks[8], (L, 2, D, 2 * D_FF), D ** -0.5),
        'ffn_w2': nrm(ks[9], (L, 2, D_FF, D), D_FF ** -0.5),
        'w_in': nrm(ks[10], (L, D, D_IN), D ** -0.5),
        'w_out': nrm(ks[11], (L, D_MIX, D), D_MIX ** -0.5),
        'lru_conv_w': nrm(ks[12], (L, LRU_CONV, LRU_W), LRU_CONV ** -0.5),
        'lru_conv_b': nrm(ks[13], (L, LRU_W), 0.02),
        'lru_gate_w': nrm(ks[14], (L, 2, 2, LRU_HEADS, LRU_HD, LRU_HD), LRU_HD ** -0.5),
        'lru_gate_b': nrm(ks[15], (L, 2, 2, LRU_W), 0.02),
        'lru_lambda': jnp.log(p0) - jnp.log1p(-p0),
        'attn_qk_g': 1.0 + nrm(ks[17], (L, 2, ATT_HD), 0.05),
        'gla_gate_w': nrm(ks[18], (L, 2, GLA_RANK, GLA_HEADS * GLA_DK), GLA_RANK ** -0.5),
        'gla_gate_b': nrm(ks[19], (L, 2, GLA_HEADS * GLA_DK), 0.1),
        'gla_out_g': 1.0 + nrm(ks[20], (L, GLA_DV), 0.05),
    }


def reference(x, c, ctx, c_ctx, mod_w, mod_b, norm_g, final_g, ffn_w13, ffn_w2, w_in, w_out,
              lru_conv_w, lru_conv_b, lru_gate_w, lru_gate_b, lru_lambda, attn_qk_g,
              gla_gate_w, gla_gate_b, gla_out_g):
    B, S, D = x.shape
    ROWS = S // GRID_W
    rows = jnp.repeat(jnp.arange(ROWS, dtype=jnp.int32), GRID_W)
    cols = jnp.tile(jnp.arange(GRID_W, dtype=jnp.int32), ROWS)
    sc = jax.nn.silu(c)
    scc = jax.nn.silu(c_ctx)
    xl, xc = x, ctx
    for i in range(DEPTH):
        mod_l = (sc @ mod_w[i] + mod_b[i]).reshape(B, 1, N_MOD, D)
        mod_c = (scc @ mod_w[i] + mod_b[i]).reshape(N_MOD, D)
        xl, xc = trunk_layer(xl, xc, mod_l, mod_c, norm_g[i], ffn_w13[i], ffn_w2[i], w_in[i], w_out[i],
                             lru_conv_w[i], lru_conv_b[i], lru_gate_w[i], lru_gate_b[i], lru_lambda[i],
                             attn_qk_g[i], gla_gate_w[i], gla_gate_b[i], gla_out_g[i],
                             rows, cols, i < DEPTH - 1)
    return rmsnorm(xl, final_g)
```

```python
import functools

import numpy as np
import jax
import jax.numpy as jnp
from jax import lax
from jax.experimental import pallas as pl
from jax.experimental.pallas import tpu as pltpu

F32 = jnp.float32
BF16 = jnp.bfloat16

D = 2048
S = 8192
TCX = 256
NT = S + TCX
DEPTH = 4
GRID_W = 64
EPS = 1e-6
D_FF = 5632
N_MOD = 9
DG = 512
LRU_HEADS = 8
LRU_HD = 64
LRU_C = 8.0
ATT_HD = 128
ATT_KVH = 2
ROPE_THETA = 10000.0
GLA_HEADS = 4
GLA_DK = 64
GLA_DV = 128
GLA_RANK = 16
GLA_TAU = 16.0
GLA_CHUNK = 64
FNET_GROUPS = 4

D_IN_MAIN = 3584
COL_FNET = 3584
COL_LR = 4096
D_IN_PAD = 4608

VMEM_LIMIT = 56 * 1024 * 1024
TM = 1056


def _cparams(sem):
    return pltpu.CompilerParams(dimension_semantics=sem, vmem_limit_bytes=VMEM_LIMIT)


def _mod_kernel(s_ref, w_ref, b_ref, o_ref):
    o_ref[...] = jnp.dot(s_ref[...].astype(BF16), w_ref[...].astype(BF16),
                         preferred_element_type=F32) + b_ref[...]


def _mod_all(s8, mod_w, mod_b):
    tn = 1024
    return pl.pallas_call(
        _mod_kernel,
        grid=(DEPTH, N_MOD * D // tn),
        in_specs=[pl.BlockSpec((8, D), lambda l, j: (0, 0)),
                  pl.BlockSpec((None, D, tn), lambda l, j: (l, 0, j)),
                  pl.BlockSpec((None, 1, tn), lambda l, j: (l, 0, j))],
        out_specs=pl.BlockSpec((None, 8, tn), lambda l, j: (l, 0, j)),
        out_shape=jax.ShapeDtypeStruct((DEPTH, 8, N_MOD * D), F32),
        compiler_params=_cparams(("arbitrary", "arbitrary")),
    )(s8, mod_w, mod_b.reshape(DEPTH, 1, N_MOD * D))


def _row_select(m, row0, tm, lat_row, ctx_row):
    rows = row0 + lax.broadcasted_iota(jnp.int32, (tm, 1), 0)
    return jnp.where(rows >= S, m[ctx_row:ctx_row + 1, :], m[lat_row:lat_row + 1, :])


ROW_CHUNK = 176


def _norm_mod_store(x_ref, g_ref, m_ref, xn_ref, row0, tm):
    g = g_ref[...]
    m = m_ref[...]

    def body(c, carry):
        r = pl.multiple_of(c * ROW_CHUNK, 16)
        x = x_ref[pl.ds(r, ROW_CHUNK), :]
        y = x * lax.rsqrt(jnp.mean(x * x, axis=-1, keepdims=True) + EPS) * g
        shift = _row_select(m, row0 + r, ROW_CHUNK, 0, 3)
        scale = _row_select(m, row0 + r, ROW_CHUNK, 1, 4)
        xn_ref[pl.ds(r, ROW_CHUNK), :] = (y * (1.0 + scale) + shift).astype(BF16)
        return carry

    lax.fori_loop(0, tm // ROW_CHUNK, body, 0)


def _ffn_a_kernel(x_ref, g_ref, m_ref, wg_ref, wu_ref, h_ref, xn_ref, *, tm):
    @pl.when(pl.program_id(1) == 0)
    def _():
        _norm_mod_store(x_ref, g_ref, m_ref, xn_ref, pl.program_id(0) * tm, tm)

    xn = xn_ref[...]
    gate = jnp.dot(xn, wg_ref[...].astype(BF16), preferred_element_type=F32)
    up = jnp.dot(xn, wu_ref[...].astype(BF16), preferred_element_type=F32)
    h_ref[...] = (gate * jax.nn.sigmoid(gate) * up).astype(BF16)


def _ffn_a(x, g, m, ffn_w13, l, a):
    tm, tf = TM, 512
    nf = D_FF // tf
    return pl.pallas_call(
        functools.partial(_ffn_a_kernel, tm=tm),
        grid=(NT // tm, nf),
        in_specs=[pl.BlockSpec((tm, D), lambda i, j: (i, 0)),
                  pl.BlockSpec((1, D), lambda i, j: (0, 0)),
                  pl.BlockSpec((8, D), lambda i, j: (0, 0)),
                  pl.BlockSpec((None, None, D, tf), lambda i, j: (l, a, 0, j)),
                  pl.BlockSpec((None, None, D, tf), lambda i, j: (l, a, 0, j + nf))],
        out_specs=pl.BlockSpec((tm, tf), lambda i, j: (i, j)),
        out_shape=jax.ShapeDtypeStruct((NT, D_FF), BF16),
        scratch_shapes=[pltpu.VMEM((tm, D), BF16)],
        compiler_params=_cparams(("arbitrary", "arbitrary")),
    )(x, g, m, ffn_w13, ffn_w13)


def _ffn_b_kernel(h_ref, w_ref, x_ref, m_ref, o_ref, *, tm, nk, coef):
    k = pl.program_id(2)
    part = jnp.dot(h_ref[...], w_ref[...].astype(BF16), preferred_element_type=F32)

    @pl.when(k == 0)
    def _():
        o_ref[...] = part

    @pl.when(k > 0)
    def _():
        o_ref[...] += part

    @pl.when(k == nk - 1)
    def _():
        m = m_ref[...]
        row0 = pl.program_id(0) * tm

        def body(c, carry):
            r = pl.multiple_of(c * ROW_CHUNK, 16)
            gate = _row_select(m, row0 + r, ROW_CHUNK, 2, 5)
            rows = pl.ds(r, ROW_CHUNK)
            o_ref[rows, :] = x_ref[rows, :] + (coef * gate) * o_ref[rows, :]
            return carry

        lax.fori_loop(0, tm // ROW_CHUNK, body, 0)


def _ffn_b(h, x, m, ffn_w2, l, a):
    tm, tn, tk = TM, 1024, 1408
    nk = D_FF // tk
    return pl.pallas_call(
        functools.partial(_ffn_b_kernel, tm=tm, nk=nk, coef=0.5),
        grid=(NT // tm, D // tn, nk),
        in_specs=[pl.BlockSpec((tm, tk), lambda i, n, k: (i, k)),
                  pl.BlockSpec((None, None, tk, tn), lambda i, n, k: (l, a, k, n)),
                  pl.BlockSpec((tm, tn), lambda i, n, k: (i, n)),
                  pl.BlockSpec((8, tn), lambda i, n, k: (0, n))],
        out_specs=pl.BlockSpec((tm, tn), lambda i, n, k: (i, n)),
        out_shape=jax.ShapeDtypeStruct((NT, D), F32),
        compiler_params=_cparams(("arbitrary", "arbitrary", "arbitrary")),
    )(h, ffn_w2, x, m)


def _proj_kernel(x_ref, g_ref, m_ref, w_ref, p_ref, xn_ref, *, tm):
    @pl.when(pl.program_id(1) == 0)
    def _():
        _norm_mod_store(x_ref, g_ref, m_ref, xn_ref, pl.program_id(0) * tm, tm)

    p_ref[...] = jnp.dot(xn_ref[...], w_ref[...].astype(BF16), preferred_element_type=F32)


def _proj_in(x, g, m, w_in_p):
    tm, tn = TM, 512
    return pl.pallas_call(
        functools.partial(_proj_kernel, tm=tm),
        grid=(NT // tm, D_IN_PAD // tn),
        in_specs=[pl.BlockSpec((tm, D), lambda i, j: (i, 0)),
                  pl.BlockSpec((1, D), lambda i, j: (0, 0)),
                  pl.BlockSpec((8, D), lambda i, j: (0, 0)),
                  pl.BlockSpec((D, tn), lambda i, j: (0, j))],
        out_specs=pl.BlockSpec((tm, tn), lambda i, j: (i, j)),
        out_shape=jax.ShapeDtypeStruct((NT, D_IN_PAD), F32),
        scratch_shapes=[pltpu.VMEM((tm, D), BF16)],
        compiler_params=_cparams(("arbitrary", "arbitrary")),
    )(x, g, m, w_in_p)


def _proj_out_kernel(ya_ref, yb_ref, yc_ref, yd_ref, w_ref, x_ref, m_ref, o_ref, *, tm):
    acc = jnp.dot(ya_ref[...], w_ref[0 * DG:1 * DG, :].astype(BF16), preferred_element_type=F32)
    acc += jnp.dot(yb_ref[...], w_ref[1 * DG:2 * DG, :].astype(BF16), preferred_element_type=F32)
    acc += jnp.dot(yc_ref[...], w_ref[2 * DG:3 * DG, :].astype(BF16), preferred_element_type=F32)
    acc += jnp.dot(yd_ref[...], w_ref[3 * DG:4 * DG, :].astype(BF16), preferred_element_type=F32)
    gate = _row_select(m_ref[...], pl.program_id(0) * tm, tm, 2, 5)
    o_ref[...] = x_ref[...] + gate * acc


def _proj_out(ya, yb, yc, yd, x, m, w_out, l):
    tm, tn = TM, 512
    yspec = pl.BlockSpec((tm, DG), lambda i, j: (i, 0))
    return pl.pallas_call(
        functools.partial(_proj_out_kernel, tm=tm),
        grid=(NT // tm, D // tn),
        in_specs=[yspec, yspec, yspec, yspec,
                  pl.BlockSpec((None, D, tn), lambda i, j: (l, 0, j)),
                  pl.BlockSpec((tm, tn), lambda i, j: (i, j)),
                  pl.BlockSpec((8, tn), lambda i, j: (0, j))],
        out_specs=pl.BlockSpec((tm, tn), lambda i, j: (i, j)),
        out_shape=jax.ShapeDtypeStruct((NT, D), F32),
        compiler_params=_cparams(("arbitrary", "arbitrary")),
    )(ya, yb, yc, yd, w_out, x, m)


def _final_norm_kernel(x_ref, g_ref, o_ref):
    x = x_ref[...]
    o_ref[...] = x * lax.rsqrt(jnp.mean(x * x, axis=-1, keepdims=True) + EPS) * g_ref[...]


def _final_norm(x, g):
    tm = 1024
    return pl.pallas_call(
        _final_norm_kernel,
        grid=(S // tm,),
        in_specs=[pl.BlockSpec((tm, D), lambda i: (i, 0)),
                  pl.BlockSpec((1, D), lambda i: (0, 0))],
        out_specs=pl.BlockSpec((tm, D), lambda i: (i, 0)),
        out_shape=jax.ShapeDtypeStruct((S, D), F32),
        compiler_params=_cparams(("arbitrary",)),
    )(x, g)


def _rms_rope(t, g, cos, sin_signed, scale):
    y = t * lax.rsqrt(jnp.mean(t * t, axis=-1, keepdims=True) + EPS) * g
    lane = lax.broadcasted_iota(jnp.int32, y.shape, 1)
    partner = jnp.where((lane % 64) < 32, pltpu.roll(y, 96, 1), pltpu.roll(y, 32, 1))
    out = y * cos + partner * sin_signed
    return out * scale if scale != 1.0 else out


def _attn_prep_kernel(q_ref, k_ref, v_ref, cos_ref, sin_ref, g_ref, qo_ref, ko_ref, vo_ref):
    cos = cos_ref[...]
    sin = sin_ref[...]
    gq = g_ref[0:1, :]
    gk = g_ref[1:2, :]
    for h in range(4):
        sl = slice(h * ATT_HD, (h + 1) * ATT_HD)
        qo_ref[:, sl] = _rms_rope(q_ref[:, sl], gq, cos, sin, ATT_HD ** -0.5).astype(BF16)
    for h in range(ATT_KVH):
        sl = slice(h * ATT_HD, (h + 1) * ATT_HD)
        ko_ref[:, sl] = _rms_rope(k_ref[:, sl], gk, cos, sin, 1.0).astype(BF16)
    vo_ref[...] = v_ref[...].astype(BF16)


def _attn_prep(p, cos_t, sin_t, qk_g):
    tm = TM
    return pl.pallas_call(
        _attn_prep_kernel,
        grid=(NT // tm,),
        in_specs=[pl.BlockSpec((tm, 512), lambda i: (i, 2)),
                  pl.BlockSpec((tm, 256), lambda i: (i, 6)),
                  pl.BlockSpec((tm, 256), lambda i: (i, 7)),
                  pl.BlockSpec((tm, ATT_HD), lambda i: (i, 0)),
                  pl.BlockSpec((tm, ATT_HD), lambda i: (i, 0)),
                  pl.BlockSpec((2, ATT_HD), lambda i: (0, 0))],
        out_specs=[pl.BlockSpec((tm, 512), lambda i: (i, 0)),
                   pl.BlockSpec((tm, 256), lambda i: (i, 0)),
                   pl.BlockSpec((tm, 256), lambda i: (i, 0))],
        out_shape=[jax.ShapeDtypeStruct((NT, 512), BF16),
                   jax.ShapeDtypeStruct((NT, 256), BF16),
                   jax.ShapeDtypeStruct((NT, 256), BF16)],
        compiler_params=_cparams(("arbitrary",)),
    )(p, p, p, cos_t, sin_t, qk_g)


def _flash_kernel(q_ref, k_ref, v_ref, o_ref, m_ref, l_ref, acc_ref, *, tq, nkv):
    j = pl.program_id(2)

    @pl.when(j == 0)
    def _():
        m_ref[...] = jnp.full_like(m_ref, -jnp.inf)
        l_ref[...] = jnp.zeros_like(l_ref)
        acc_ref[...] = jnp.zeros_like(acc_ref)

    q2 = jnp.concatenate([q_ref[:, :ATT_HD], q_ref[:, ATT_HD:]], axis=0)
    s = lax.dot_general(q2, k_ref[...], (((1,), (1,)), ((), ())), preferred_element_type=F32)
    m_prev = m_ref[...]
    m_new = jnp.maximum(m_prev, jnp.max(s, axis=-1, keepdims=True))
    alpha = jnp.exp(m_prev - m_new)
    p = jnp.exp(s - m_new)
    l_ref[...] = alpha * l_ref[...] + jnp.sum(p, axis=-1, keepdims=True)
    acc_ref[...] = alpha * acc_ref[...] + jnp.dot(p.astype(BF16), v_ref[...],
                                                 preferred_element_type=F32)
    m_ref[...] = m_new

    @pl.when(j == nkv - 1)
    def _():
        o = acc_ref[...] / l_ref[...]
        o_ref[:, :ATT_HD] = o[:tq].astype(o_ref.dtype)
        o_ref[:, ATT_HD:] = o[tq:].astype(o_ref.dtype)


def _flash(qn, kn, vb, n_q, tq, q_blk0, n_kv, tk, kv_blk0):
    return pl.pallas_call(
        functools.partial(_flash_kernel, tq=tq, nkv=n_kv),
        grid=(ATT_KVH, n_q, n_kv),
        in_specs=[pl.BlockSpec((tq, 2 * ATT_HD), lambda h, i, j: (i + q_blk0, h)),
                  pl.BlockSpec((tk, ATT_HD), lambda h, i, j: (j + kv_blk0, h)),
                  pl.BlockSpec((tk, ATT_HD), lambda h, i, j: (j + kv_blk0, h))],
        out_specs=pl.BlockSpec((tq, 2 * ATT_HD), lambda h, i, j: (i, h)),
        out_shape=jax.ShapeDtypeStruct((n_q * tq, 4 * ATT_HD), BF16),
        scratch_shapes=[pltpu.VMEM((2 * tq, 1), F32),
                        pltpu.VMEM((2 * tq, 1), F32),
                        pltpu.VMEM((2 * tq, ATT_HD), F32)],
        compiler_params=_cparams(("arbitrary", "arbitrary", "arbitrary")),
    )(qn, kn, vb)


def _attn_mixer(p, cos_t, sin_t, qk_g):
    qn, kn, vb = _attn_prep(p, cos_t, sin_t, qk_g)
    yl = _flash(qn, kn, vb, S // 1024, 1024, 0, NT // 768, 768, 0)
    yc = _flash(qn, kn, vb, 1, TCX, S // TCX, 1, TCX, S // TCX)
    return jnp.concatenate([yl, yc], axis=0)


def _dwconv(u, w, b):
    T = u.shape[0]
    up = jnp.pad(u, ((2, 1), (0, 0)))
    return b + sum(w[j] * up[j:j + T] for j in range(4))


def _lin_comb(l, r):
    return l[0] * r[0], r[0] * l[1] + r[1]


def _scan_fwd(a, b, h0):
    b = b.at[0].add(a[0] * h0)
    _, h = lax.associative_scan(_lin_comb, (a, b), axis=0)
    return h


def _rglru_coeffs(u, w_gate, b_gate, lam):
    T = u.shape[0]
    uh = u.reshape(T, LRU_HEADS, LRU_HD)
    gates = jnp.einsum('thi,ghij->gthj', uh, w_gate).reshape(2, T, DG) + b_gate[:, None, :]
    r = jax.nn.sigmoid(gates[0])
    i = jax.nn.sigmoid(gates[1])
    log_a = -LRU_C * r * jax.nn.softplus(-lam)
    a = jnp.exp(log_a)
    return a, jnp.sqrt(-jnp.expm1(2.0 * log_a)) * (i * u)


def _lru_jax(p, conv_w, conv_b, w_gate, b_gate, lam):
    x = p[:, 0:DG]
    g = p[:, DG:2 * DG]
    ul = _dwconv(x[:S], conv_w, conv_b)
    uc = _dwconv(x[S:], conv_w, conv_b)
    hl_sum = 0.0
    hc_sum = 0.0
    for d in range(2):
        fl = (lambda t: jnp.flip(t, 0)) if d else (lambda t: t)
        ac, bc = _rglru_coeffs(fl(uc), w_gate[d], b_gate[d], lam[d])
        hc = _scan_fwd(ac, bc, jnp.zeros_like(uc[0]))
        al, bl = _rglru_coeffs(fl(ul), w_gate[d], b_gate[d], lam[d])
        hl = _scan_fwd(al, bl, hc[-1])
        hl_sum = hl_sum + fl(hl)
        hc_sum = hc_sum + fl(hc)
    h = jnp.concatenate([hl_sum, hc_sum], axis=0)
    return (h * jax.nn.gelu(g)).astype(BF16)


def _head_rms(t, g):
    return t * lax.rsqrt(jnp.mean(t * t, axis=-1, keepdims=True) + EPS) * g


def _gla_chunked(q, k, v, log_a, s0):
    T, H, DK = q.shape
    C = GLA_CHUNK
    N = T // C
    rs = lambda t: t.reshape(N, C, H, t.shape[-1])
    q, k, v, log_a = rs(q), rs(k), rs(v), rs(log_a)
    bcum = jnp.cumsum(log_a, axis=1)
    blast = bcum[:, -1:]
    q_t = q * jnp.exp(bcum)
    k_in = k * jnp.exp(-bcum)
    k_out = k * jnp.exp(blast - bcum)
    mask = jnp.tril(jnp.ones((C, C), dtype=bool))
    att = jnp.where(mask, jnp.einsum('nthd,nshd->nhts', q_t, k_in), 0.0)
    o_intra = jnp.einsum('nhts,nshv->nthv', att, v)
    d_state = jnp.einsum('nshd,nshv->nhdv', k_out, v)
    decay = jnp.exp(blast[:, 0])

    def step(s, inp):
        dec, ds = inp
        return dec[..., None] * s + ds, s

    s_final, s_prev = lax.scan(step, s0, (decay, d_state))
    o_inter = jnp.einsum('nthd,nhdv->nthv', q_t, s_prev)
    return (o_intra + o_inter).reshape(T, H, -1), s_final


def _gla_jax(p, gate_w, gate_b, out_g):
    def prep(rows):
        T = rows.shape[0]
        hk = lambda t: t.reshape(T, GLA_HEADS, -1)
        q = rows[:, 2048:2304]
        k = rows[:, 2304:2560]
        v = rows[:, 2560:3072]
        g = rows[:, 3072:3584]
        lr = rows[:, COL_LR:COL_LR + 2 * GLA_RANK]
        log_a = [hk(jax.nn.log_sigmoid(lr[:, d * GLA_RANK:(d + 1) * GLA_RANK] @ gate_w[d] + gate_b[d])
                    / GLA_TAU) for d in range(2)]
        return hk(q) * (GLA_DK ** -0.5), hk(k), hk(v), log_a, hk(g)

    ql, kl, vl, lal, gl = prep(p[:S])
    qc, kc, vc, lac, gc = prep(p[S:])
    s0 = jnp.zeros((GLA_HEADS, GLA_DK, GLA_DV), F32)
    ol_sum = 0.0
    oc_sum = 0.0
    for d in range(2):
        fl = (lambda t: jnp.flip(t, 0)) if d else (lambda t: t)
        oc, sc = _gla_chunked(fl(qc), fl(kc), fl(vc), fl(lac[d]), s0)
        ol, _ = _gla_chunked(fl(ql), fl(kl), fl(vl), fl(lal[d]), sc)
        ol_sum = ol_sum + fl(ol)
        oc_sum = oc_sum + fl(oc)
    o = jnp.concatenate([ol_sum, oc_sum], axis=0)
    g = jnp.concatenate([gl, gc], axis=0)
    return (_head_rms(o, out_g) * jax.nn.silu(g)).reshape(NT, DG).astype(BF16)


def _fnet_jax(p):
    def mix(u):
        T = u.shape[0]
        uf = u.reshape(T, FNET_GROUPS, DG // FNET_GROUPS)
        return jnp.fft.fft2(uf, axes=(0, 2), norm='ortho').real.reshape(T, DG)

    u = p[:, COL_FNET:COL_FNET + DG]
    return jnp.concatenate([mix(u[:S]), mix(u[S:])], axis=0).astype(BF16)


def _rope_tables():
    t = np.arange(S)
    half = 32
    freqs = ROPE_THETA ** (-np.arange(half, dtype=np.float64) / half)
    ang_r = (t // GRID_W)[:, None] * freqs
    ang_c = (t % GRID_W)[:, None] * freqs
    cos = np.concatenate([np.cos(ang_r), np.cos(ang_r), np.cos(ang_c), np.cos(ang_c)], axis=1)
    sin = np.concatenate([-np.sin(ang_r), np.sin(ang_r), -np.sin(ang_c), np.sin(ang_c)], axis=1)
    cos = np.concatenate([cos, np.ones((TCX, ATT_HD))], axis=0)
    sin = np.concatenate([sin, np.zeros((TCX, ATT_HD))], axis=0)
    return jnp.asarray(cos, F32), jnp.asarray(sin, F32)


def _mod_rows(mod, j):
    m = jnp.concatenate([mod[0, 3 * j:3 * j + 3], mod[1, 3 * j:3 * j + 3],
                         jnp.zeros((2, D), F32)], axis=0)
    return m


def kernel(x, c, ctx, c_ctx, mod_w, mod_b, norm_g, final_g, ffn_w13, ffn_w2, w_in, w_out,
           lru_conv_w, lru_conv_b, lru_gate_w, lru_gate_b, lru_lambda, attn_qk_g,
           gla_gate_w, gla_gate_b, gla_out_g):
    xs = jnp.concatenate([x[0], ctx[0]], axis=0)
    s8 = jnp.concatenate([jax.nn.silu(c), jax.nn.silu(c_ctx)[None, :],
                          jnp.zeros((6, D), F32)], axis=0)
    mods = _mod_all(s8, mod_w, mod_b).reshape(DEPTH, 8, N_MOD, D)
    w_in_p = jnp.concatenate([w_in[:, :, :D_IN_MAIN], w_in[:, :, D_IN_MAIN + 32:],
                              w_in[:, :, D_IN_MAIN:D_IN_MAIN + 32],
                              jnp.zeros((DEPTH, D, D_IN_PAD - COL_LR - 32), F32)], axis=2)
    cos_t, sin_t = _rope_tables()

    for l in range(DEPTH):
        mod = mods[l]
        g = norm_g[l]
        m0, m1, m2 = _mod_rows(mod, 0), _mod_rows(mod, 1), _mod_rows(mod, 2)
        h = _ffn_a(xs, g[0:1], m0, ffn_w13, l, 0)
        xs = _ffn_b(h, xs, m0, ffn_w2, l, 0)
        p = _proj_in(xs, g[1:2], m1, w_in_p[l])
        ya = _lru_jax(p, lru_conv_w[l], lru_conv_b[l], lru_gate_w[l], lru_gate_b[l], lru_lambda[l])
        yb = _attn_mixer(p, cos_t, sin_t, attn_qk_g[l])
        yc = _gla_jax(p, gla_gate_w[l], gla_gate_b[l], gla_out_g[l])
        yd = _fnet_jax(p)
        xs = _proj_out(ya, yb, yc, yd, xs, m1, w_out, l)
        h = _ffn_a(xs, g[2:3], m2, ffn_w13, l, 1)
        xs = _ffn_b(h, xs, m2, ffn_w2, l, 1)
    return _final_norm(xs, final_g[None, :])[None]
```

```python
import functools

import numpy as np
import jax
import jax.numpy as jnp
from jax import lax
from jax.experimental import pallas as pl
from jax.experimental.pallas import tpu as pltpu

F32 = jnp.float32
BF16 = jnp.bfloat16

D = 2048
S = 8192
TCX = 256
NT = S + TCX
DEPTH = 4
GRID_W = 64
EPS = 1e-6
D_FF = 5632
N_MOD = 9
DG = 512
LRU_HEADS = 8
LRU_HD = 64
LRU_C = 8.0
ATT_HD = 128
ATT_KVH = 2
ROPE_THETA = 10000.0
GLA_HEADS = 4
GLA_DK = 64
GLA_DV = 128
GLA_RANK = 16
GLA_TAU = 16.0
GLA_CHUNK = 64
FNET_CH = 128

D_IN_MAIN = 3584
COL_LR = 3584
D_P = 4096
D_IN_PAD = 4608

VMEM_LIMIT = 56 * 1024 * 1024
TM = 1056
TB = 256
ROW_CHUNK = 176


def _cparams(sem):
    return pltpu.CompilerParams(dimension_semantics=sem, vmem_limit_bytes=VMEM_LIMIT)


def _dot(a, b):
    return jnp.dot(a, b, preferred_element_type=F32)


def _dot_nt(a, b):
    return lax.dot_general(a, b, (((1,), (1,)), ((), ())), preferred_element_type=F32)


def _dot_tn(a, b):
    return lax.dot_general(a, b, (((0,), (0,)), ((), ())), preferred_element_type=F32)


def _softplus(x):
    return jnp.maximum(x, 0.0) + jnp.log(1.0 + jnp.exp(-jnp.abs(x)))


def _mod_kernel(s_ref, w_ref, b_ref, o_ref):
    o_ref[...] = _dot(s_ref[...].astype(BF16), w_ref[...].astype(BF16)) + b_ref[...]


def _mod_all(s8, mod_w, mod_b):
    tn = 1024
    return pl.pallas_call(
        _mod_kernel,
        grid=(DEPTH, N_MOD * D // tn),
        in_specs=[pl.BlockSpec((8, D), lambda l, j: (0, 0)),
                  pl.BlockSpec((None, D, tn), lambda l, j: (l, 0, j)),
                  pl.BlockSpec((None, 1, tn), lambda l, j: (l, 0, j))],
        out_specs=pl.BlockSpec((None, 8, tn), lambda l, j: (l, 0, j)),
        out_shape=jax.ShapeDtypeStruct((DEPTH, 8, N_MOD * D), F32),
        compiler_params=_cparams(("arbitrary", "arbitrary")),
    )(s8, mod_w, mod_b.reshape(DEPTH, 1, N_MOD * D))


def _row_select(m, row0, tm, lat_row, ctx_row):
    rows = row0 + lax.broadcasted_iota(jnp.int32, (tm, 1), 0)
    return jnp.where(rows >= S, m[ctx_row:ctx_row + 1, :], m[lat_row:lat_row + 1, :])


def _norm_mod_store(x_ref, g_ref, m_ref, xn_ref, row0, tm):
    g = g_ref[...]
    m = m_ref[...]

    def body(c, carry):
        r = pl.multiple_of(c * ROW_CHUNK, 16)
        x = x_ref[pl.ds(r, ROW_CHUNK), :]
        y = x * lax.rsqrt(jnp.mean(x * x, axis=-1, keepdims=True) + EPS) * g
        shift = _row_select(m, row0 + r, ROW_CHUNK, 0, 3)
        scale = _row_select(m, row0 + r, ROW_CHUNK, 1, 4)
        xn_ref[pl.ds(r, ROW_CHUNK), :] = (y * (1.0 + scale) + shift).astype(BF16)
        return carry

    lax.fori_loop(0, tm // ROW_CHUNK, body, 0)


def _ffn_a_kernel(x_ref, g_ref, m_ref, wg_ref, wu_ref, h_ref, xn_ref, *, tm):
    @pl.when(pl.program_id(1) == 0)
    def _():
        _norm_mod_store(x_ref, g_ref, m_ref, xn_ref, pl.program_id(0) * tm, tm)

    xn = xn_ref[...]
    gate = _dot(xn, wg_ref[...].astype(BF16))
    up = _dot(xn, wu_ref[...].astype(BF16))
    h_ref[...] = (gate * jax.nn.sigmoid(gate) * up).astype(BF16)


def _ffn_a(x, g, m, ffn_w13, l, a):
    tm, tf = TM, 512
    nf = D_FF // tf
    return pl.pallas_call(
        functools.partial(_ffn_a_kernel, tm=tm),
        grid=(NT // tm, nf),
        in_specs=[pl.BlockSpec((tm, D), lambda i, j: (i, 0)),
                  pl.BlockSpec((1, D), lambda i, j: (0, 0)),
                  pl.BlockSpec((8, D), lambda i, j: (0, 0)),
                  pl.BlockSpec((None, None, D, tf), lambda i, j: (l, a, 0, j)),
                  pl.BlockSpec((None, None, D, tf), lambda i, j: (l, a, 0, j + nf))],
        out_specs=pl.BlockSpec((tm, tf), lambda i, j: (i, j)),
        out_shape=jax.ShapeDtypeStruct((NT, D_FF), BF16),
        scratch_shapes=[pltpu.VMEM((tm, D), BF16)],
        compiler_params=_cparams(("arbitrary", "arbitrary")),
    )(x, g, m, ffn_w13, ffn_w13)


def _ffn_b_kernel(h_ref, w_ref, x_ref, m_ref, o_ref, *, tm, nk, coef):
    k = pl.program_id(2)
    part = _dot(h_ref[...], w_ref[...].astype(BF16))

    @pl.when(k == 0)
    def _():
        o_ref[...] = part

    @pl.when(k > 0)
    def _():
        o_ref[...] += part

    @pl.when(k == nk - 1)
    def _():
        m = m_ref[...]
        row0 = pl.program_id(0) * tm

        def body(c, carry):
            r = pl.multiple_of(c * ROW_CHUNK, 16)
            gate = _row_select(m, row0 + r, ROW_CHUNK, 2, 5)
            rows = pl.ds(r, ROW_CHUNK)
            o_ref[rows, :] = x_ref[rows, :] + (coef * gate) * o_ref[rows, :]
            return carry

        lax.fori_loop(0, tm // ROW_CHUNK, body, 0)


def _ffn_b(h, x, m, ffn_w2, l, a):
    tm, tn, tk = TM, 1024, 1408
    nk = D_FF // tk
    return pl.pallas_call(
        functools.partial(_ffn_b_kernel, tm=tm, nk=nk, coef=0.5),
        grid=(NT // tm, D // tn, nk),
        in_specs=[pl.BlockSpec((tm, tk), lambda i, n, k: (i, k)),
                  pl.BlockSpec((None, None, tk, tn), lambda i, n, k: (l, a, k, n)),
                  pl.BlockSpec((tm, tn), lambda i, n, k: (i, n)),
                  pl.BlockSpec((8, tn), lambda i, n, k: (0, n))],
        out_specs=pl.BlockSpec((tm, tn), lambda i, n, k: (i, n)),
        out_shape=jax.ShapeDtypeStruct((NT, D), F32),
        compiler_params=_cparams(("arbitrary", "arbitrary", "arbitrary")),
    )(h, ffn_w2, x, m)


def _proj_kernel(x_ref, g_ref, m_ref, w_ref, p_ref, u_ref, xn_ref, *, tm, n_main):
    j = pl.program_id(1)

    @pl.when(j == 0)
    def _():
        _norm_mod_store(x_ref, g_ref, m_ref, xn_ref, pl.program_id(0) * tm, tm)

    res = _dot(xn_ref[...], w_ref[...].astype(BF16))

    @pl.when(j < n_main)
    def _():
        p_ref[...] = res

    @pl.when(j == n_main)
    def _():
        u_ref[...] = res


def _proj_in(x, g, m, w_in_p):
    tm, tn = TM, 512
    n_main = D_P // tn
    return pl.pallas_call(
        functools.partial(_proj_kernel, tm=tm, n_main=n_main),
        grid=(NT // tm, n_main + 1),
        in_specs=[pl.BlockSpec((tm, D), lambda i, j: (i, 0)),
                  pl.BlockSpec((1, D), lambda i, j: (0, 0)),
                  pl.BlockSpec((8, D), lambda i, j: (0, 0)),
                  pl.BlockSpec((D, tn), lambda i, j: (0, j))],
        out_specs=[pl.BlockSpec((tm, tn), lambda i, j: (i, jnp.minimum(j, n_main - 1))),
                   pl.BlockSpec((tm, DG), lambda i, j: (i, 0))],
        out_shape=[jax.ShapeDtypeStruct((NT, D_P), F32),
                   jax.ShapeDtypeStruct((NT, DG), F32)],
        scratch_shapes=[pltpu.VMEM((tm, D), BF16)],
        compiler_params=_cparams(("arbitrary", "arbitrary")),
    )(x, g, m, w_in_p)


def _proj_out_kernel(ya_ref, yb_ref, yc_ref, yd_ref, w_ref, x_ref, m_ref, o_ref, *, tm):
    acc = _dot(ya_ref[...], w_ref[0 * DG:1 * DG, :].astype(BF16))
    acc += _dot(yb_ref[...], w_ref[1 * DG:2 * DG, :].astype(BF16))
    acc += _dot(yc_ref[...], w_ref[2 * DG:3 * DG, :].astype(BF16))
    acc += _dot(yd_ref[...], w_ref[3 * DG:4 * DG, :].astype(BF16))
    gate = _row_select(m_ref[...], pl.program_id(0) * tm, tm, 2, 5)
    o_ref[...] = x_ref[...] + gate * acc


def _proj_out(ya, yb, yc, yd, x, m, w_out, l):
    tm, tn = TM, 512
    yspec = pl.BlockSpec((tm, DG), lambda i, j: (i, 0))
    return pl.pallas_call(
        functools.partial(_proj_out_kernel, tm=tm),
        grid=(NT // tm, D // tn),
        in_specs=[yspec, yspec, yspec, yspec,
                  pl.BlockSpec((None, D, tn), lambda i, j: (l, 0, j)),
                  pl.BlockSpec((tm, tn), lambda i, j: (i, j)),
                  pl.BlockSpec((8, tn), lambda i, j: (0, j))],
        out_specs=pl.BlockSpec((tm, tn), lambda i, j: (i, j)),
        out_shape=jax.ShapeDtypeStruct((NT, D), F32),
        compiler_params=_cparams(("arbitrary", "arbitrary")),
    )(ya, yb, yc, yd, w_out, x, m)


def _final_norm_kernel(x_ref, g_ref, o_ref):
    x = x_ref[...]
    o_ref[...] = x * lax.rsqrt(jnp.mean(x * x, axis=-1, keepdims=True) + EPS) * g_ref[...]


def _final_norm(x, g):
    tm = 1024
    return pl.pallas_call(
        _final_norm_kernel,
        grid=(S // tm,),
        in_specs=[pl.BlockSpec((tm, D), lambda i: (i, 0)),
                  pl.BlockSpec((1, D), lambda i: (0, 0))],
        out_specs=pl.BlockSpec((tm, D), lambda i: (i, 0)),
        out_shape=jax.ShapeDtypeStruct((S, D), F32),
        compiler_params=_cparams(("arbitrary",)),
    )(x, g)


def _rms_rope(t, g, cos, sin_signed, scale):
    y = t * lax.rsqrt(jnp.mean(t * t, axis=-1, keepdims=True) + EPS) * g
    lane = lax.broadcasted_iota(jnp.int32, y.shape, 1)
    partner = jnp.where((lane % 64) < 32, pltpu.roll(y, 96, 1), pltpu.roll(y, 32, 1))
    out = y * cos + partner * sin_signed
    return out * scale if scale != 1.0 else out


def _attn_prep_kernel(q_ref, k_ref, v_ref, cos_ref, sin_ref, g_ref, qo_ref, ko_ref, vo_ref):
    cos = cos_ref[...]
    sin = sin_ref[...]
    gq = g_ref[0:1, :]
    gk = g_ref[1:2, :]
    for h in range(4):
        sl = slice(h * ATT_HD, (h + 1) * ATT_HD)
        qo_ref[:, sl] = _rms_rope(q_ref[:, sl], gq, cos, sin, ATT_HD ** -0.5).astype(BF16)
    for h in range(ATT_KVH):
        sl = slice(h * ATT_HD, (h + 1) * ATT_HD)
        ko_ref[:, sl] = _rms_rope(k_ref[:, sl], gk, cos, sin, 1.0).astype(BF16)
    vo_ref[...] = v_ref[...].astype(BF16)


def _attn_prep(p, cos_t, sin_t, qk_g):
    tm = TM
    return pl.pallas_call(
        _attn_prep_kernel,
        grid=(NT // tm,),
        in_specs=[pl.BlockSpec((tm, 512), lambda i: (i, 2)),
                  pl.BlockSpec((tm, 256), lambda i: (i, 6)),
                  pl.BlockSpec((tm, 256), lambda i: (i, 7)),
                  pl.BlockSpec((tm, ATT_HD), lambda i: (i, 0)),
                  pl.BlockSpec((tm, ATT_HD), lambda i: (i, 0)),
                  pl.BlockSpec((2, ATT_HD), lambda i: (0, 0))],
        out_specs=[pl.BlockSpec((tm, 512), lambda i: (i, 0)),
                   pl.BlockSpec((tm, 256), lambda i: (i, 0)),
                   pl.BlockSpec((tm, 256), lambda i: (i, 0))],
        out_shape=[jax.ShapeDtypeStruct((NT, 512), BF16),
                   jax.ShapeDtypeStruct((NT, 256), BF16),
                   jax.ShapeDtypeStruct((NT, 256), BF16)],
        compiler_params=_cparams(("arbitrary",)),
    )(p, p, p, cos_t, sin_t, qk_g)


def _flash_kernel(q_ref, k_ref, v_ref, *rest, tq, nkv):
    o_ref, m_ref, l_ref, acc_ref = rest[-4:]
    j = pl.program_id(2)

    @pl.when(j == 0)
    def _():
        m_ref[...] = jnp.full_like(m_ref, -jnp.inf)
        l_ref[...] = jnp.zeros_like(l_ref)
        acc_ref[...] = jnp.zeros_like(acc_ref)

    q2 = jnp.concatenate([q_ref[:, :ATT_HD], q_ref[:, ATT_HD:]], axis=0)
    s = _dot_nt(q2, k_ref[...])
    m_prev = m_ref[...]
    m_new = jnp.maximum(m_prev, jnp.max(s, axis=-1, keepdims=True))
    alpha = jnp.exp(m_prev - m_new)
    p = jnp.exp(s - m_new)
    l_ref[...] = alpha * l_ref[...] + jnp.sum(p, axis=-1, keepdims=True)
    acc_ref[...] = alpha * acc_ref[...] + _dot(p.astype(BF16), v_ref[...])
    m_ref[...] = m_new

    @pl.when(j == nkv - 1)
    def _():
        o = acc_ref[...] / l_ref[...]
        o_ref[:, :ATT_HD] = o[:tq].astype(o_ref.dtype)
        o_ref[:, ATT_HD:] = o[tq:].astype(o_ref.dtype)


def _flash(qn, kn, vb, n_q, tq, q_blk0, n_kv, tk, kv_blk0, into=None):
    in_specs = [pl.BlockSpec((tq, 2 * ATT_HD), lambda h, i, j: (i + q_blk0, h)),
                pl.BlockSpec((tk, ATT_HD), lambda h, i, j: (j + kv_blk0, h)),
                pl.BlockSpec((tk, ATT_HD), lambda h, i, j: (j + kv_blk0, h))]
    args = [qn, kn, vb]
    aliases = {}
    if into is not None:
        in_specs.append(pl.BlockSpec(memory_space=pl.ANY))
        args.append(into)
        aliases = {3: 0}
    return pl.pallas_call(
        functools.partial(_flash_kernel, tq=tq, nkv=n_kv),
        grid=(ATT_KVH, n_q, n_kv),
        in_specs=in_specs,
        out_specs=pl.BlockSpec((tq, 2 * ATT_HD), lambda h, i, j: (i + q_blk0, h)),
        out_shape=jax.ShapeDtypeStruct((NT, 4 * ATT_HD), BF16),
        scratch_shapes=[pltpu.VMEM((2 * tq, 1), F32),
                        pltpu.VMEM((2 * tq, 1), F32),
                        pltpu.VMEM((2 * tq, ATT_HD), F32)],
        input_output_aliases=aliases,
        compiler_params=_cparams(("arbitrary", "arbitrary", "arbitrary")),
    )(*args)


def _attn_mixer(p, cos_t, sin_t, qk_g):
    qn, kn, vb = _attn_prep(p, cos_t, sin_t, qk_g)
    y = _flash(qn, kn, vb, 1, TCX, S // TCX, 1, TCX, S // TCX)
    return _flash(qn, kn, vb, S // 1024, 1024, 0, NT // 768, 768, 0, into=y)


def _walk_block(s, nlat, reverse):
    if reverse:
        return jnp.where(s == 0, nlat, nlat - s)
    return jnp.where(s == 0, nlat, s - 1)


def _seq_edges(s, nlat, reverse):
    if reverse:
        return (s == 0) | (s == nlat), (s == 0) | (s == 1)
    return (s == 0) | (s == 1), (s == 0) | (s == nlat)


def _lru_kernel(xb_ref, xp_ref, xn_ref, cw_ref, cb_ref, wg_ref, bg_ref, lam_ref, *rest,
                nlat, reverse):
    if reverse:
        hf_ref, g_ref, y_ref, xe_ref, carry_ref = rest
    else:
        h_ref, xe_ref, carry_ref = rest
    s = pl.program_id(0)
    starts, ends = _seq_edges(s, nlat, reverse)

    @pl.when(s == 0)
    def _():
        carry_ref[...] = jnp.zeros_like(carry_ref)

    xe_ref[0:8, :] = jnp.where(starts, 0.0, xp_ref[...])
    xe_ref[8:8 + TB, :] = xb_ref[...]
    xe_ref[8 + TB:16 + TB, :] = jnp.where(ends, 0.0, xn_ref[...])
    u = cb_ref[...]
    for j in range(4):
        u = u + cw_ref[j:j + 1, :] * xe_ref[6 + j:6 + j + TB, :]

    gates = _dot(u.astype(BF16), wg_ref[...]) + bg_ref[...]
    r = jax.nn.sigmoid(gates[:, :DG])
    i = jax.nn.sigmoid(gates[:, DG:])
    log_a = (-LRU_C) * r * _softplus(-lam_ref[...])
    a = jnp.exp(log_a)
    b = jnp.sqrt(1.0 - jnp.exp(2.0 * log_a)) * (i * u)

    row = lax.broadcasted_iota(jnp.int32, (TB, 1), 0)
    sh = 1
    while sh < TB:
        if reverse:
            a_s, b_s, valid = pltpu.roll(a, TB - sh, 0), pltpu.roll(b, TB - sh, 0), row < TB - sh
        else:
            a_s, b_s, valid = pltpu.roll(a, sh, 0), pltpu.roll(b, sh, 0), row >= sh
        b = b + jnp.where(valid, a * b_s, 0.0)
        a = jnp.where(valid, a * a_s, a)
        sh *= 2
    h = b + a * carry_ref[0:1, :]
    carry_ref[0:1, :] = h[0:1, :] if reverse else h[TB - 1:TB, :]

    if reverse:
        g = g_ref[...]
        gelu = 0.5 * g * (1.0 + jnp.tanh(0.7978845608028654 * (g + 0.044715 * (g * g * g))))
        y_ref[...] = ((hf_ref[...] + h) * gelu).astype(BF16)
    else:
        h_ref[...] = h


def _lru_pass(p, conv_w, conv_b, wg, bg, lam, nlat, reverse, hf=None):
    blk = lambda s: _walk_block(s, nlat, reverse)
    nrow8 = (nlat + 1) * TB // 8
    in_specs = [pl.BlockSpec((TB, DG), lambda s: (blk(s), 0)),
                pl.BlockSpec((8, DG), lambda s: (jnp.maximum(blk(s) * (TB // 8) - 1, 0), 0)),
                pl.BlockSpec((8, DG), lambda s: (jnp.minimum((blk(s) + 1) * (TB // 8), nrow8 - 1), 0)),
                pl.BlockSpec((4, DG), lambda s: (0, 0)),
                pl.BlockSpec((1, DG), lambda s: (0, 0)),
                pl.BlockSpec((DG, 2 * DG), lambda s: (0, 0)),
                pl.BlockSpec((1, 2 * DG), lambda s: (0, 0)),
                pl.BlockSpec((1, DG), lambda s: (0, 0))]
    args = [p, p, p, conv_w, conv_b, wg, bg, lam]
    if reverse:
        in_specs += [pl.BlockSpec((TB, DG), lambda s: (blk(s), 0)),
                     pl.BlockSpec((TB, DG), lambda s: (blk(s), 1))]
        args += [hf, p]
    return pl.pallas_call(
        functools.partial(_lru_kernel, nlat=nlat, reverse=reverse),
        grid=(nlat + 1,),
        in_specs=in_specs,
        out_specs=pl.BlockSpec((TB, DG), lambda s: (blk(s), 0)),
        out_shape=jax.ShapeDtypeStruct(((nlat + 1) * TB, DG), BF16 if reverse else F32),
        scratch_shapes=[pltpu.VMEM((TB + 16, DG), F32), pltpu.VMEM((8, DG), F32)],
        compiler_params=_cparams(("arbitrary",)),
    )(*args)


def _lru_gate_dense(w):
    eye = jnp.eye(LRU_HEADS, dtype=w.dtype)
    dense = jnp.einsum('ghij,hk->ghikj', w, eye).reshape(2, DG, DG)
    return jnp.concatenate([dense[0], dense[1]], axis=1).astype(BF16)


def _lru_mixer(p, conv_w, conv_b, gate_w, gate_b, lam, nlat=S // TB):
    cb = conv_b[None, :]
    hf = _lru_pass(p, conv_w, cb, _lru_gate_dense(gate_w[0]), gate_b[0].reshape(1, 2 * DG),
                   lam[0:1], nlat, False)
    return _lru_pass(p, conv_w, cb, _lru_gate_dense(gate_w[1]), gate_b[1].reshape(1, 2 * DG),
                     lam[1:2], nlat, True, hf=hf)


def _split3(x):
    hi = x.astype(BF16)
    r1 = x - hi.astype(F32)
    mid = r1.astype(BF16)
    lo = (r1 - mid.astype(F32)).astype(BF16)
    return hi, mid, lo


def _gla_kernel(q_ref, k_ref, v_ref, lr_ref, wl_ref, bl_ref, *rest, reverse):
    if reverse:
        of_ref, g_ref, og_ref, y_ref, st_ref, o_acc = rest
    else:
        o_acc, st_ref = rest
    s = pl.program_id(0)
    nck = TB // GLA_CHUNK
    hw = GLA_HEADS * GLA_DK

    @pl.when(s == 0)
    def _():
        st_ref[...] = jnp.zeros_like(st_ref)

    x = _dot(lr_ref[...].astype(BF16), wl_ref[...]) + bl_ref[...]
    log_a = (jnp.minimum(x, 0.0) - jnp.log(1.0 + jnp.exp(-jnp.abs(x)))) * (1.0 / GLA_TAU)

    ri = lax.broadcasted_iota(jnp.int32, (TB, TB), 0)
    ci = lax.broadcasted_iota(jnp.int32, (TB, TB), 1)
    same = (ri // GLA_CHUNK) == (ci // GLA_CHUNK)
    causal = same & ((ci >= ri) if reverse else (ci <= ri))
    tri = jnp.where(causal, 1.0, 0.0).astype(BF16)
    ones = jnp.where(same, 1.0, 0.0).astype(BF16)
    parts = _split3(log_a)
    bcum = _dot(tri, parts[0]) + _dot(tri, parts[1]) + _dot(tri, parts[2])
    tot = _dot(ones, parts[0]) + _dot(ones, parts[1]) + _dot(ones, parts[2])

    q_t = q_ref[...] * (GLA_DK ** -0.5) * jnp.exp(bcum)
    k_in = (k_ref[...] * jnp.exp(-bcum)).astype(BF16)
    k_out = (k_ref[...] * jnp.exp(tot - bcum)).astype(BF16)
    decay = jnp.exp(tot)
    lane = lax.broadcasted_iota(jnp.int32, (1, hw), 1)

    st = st_ref[...]
    for h in range(GLA_HEADS):
        head = (lane // GLA_DK) == h
        qh = jnp.where(head, q_t, 0.0).astype(BF16)
        vh = v_ref[:, h * GLA_DV:(h + 1) * GLA_DV].astype(BF16)
        att = jnp.where(causal, _dot_nt(qh, k_in), 0.0)
        o_acc[:, h * GLA_DV:(h + 1) * GLA_DV] = _dot(att.astype(BF16), vh)
    for c in (range(nck - 1, -1, -1) if reverse else range(nck)):
        rows = slice(c * GLA_CHUNK, (c + 1) * GLA_CHUNK)
        new_st = []
        for h in range(GLA_HEADS):
            head = (lane // GLA_DK) == h
            st_h = st[h * GLA_DV:(h + 1) * GLA_DV, :]
            qh = jnp.where(head, q_t[rows, :], 0.0).astype(BF16)
            cols = slice(h * GLA_DV, (h + 1) * GLA_DV)
            o_acc[rows, cols] += _dot_nt(qh, st_h.astype(BF16))
            upd = _dot_tn(v_ref[rows, cols].astype(BF16), k_out[rows, :])
            new_st.append(decay[c * GLA_CHUNK:c * GLA_CHUNK + 1, :] * st_h + jnp.where(head, upd, 0.0))
        st = jnp.concatenate(new_st, axis=0)
    st_ref[...] = st

    if reverse:
        g = g_ref[...]
        og = og_ref[...]
        for h in range(GLA_HEADS):
            cols = slice(h * GLA_DV, (h + 1) * GLA_DV)
            o = of_ref[:, cols] + o_acc[:, cols]
            o = o * lax.rsqrt(jnp.mean(o * o, axis=-1, keepdims=True) + EPS) * og
            gh = g[:, cols]
            y_ref[:, cols] = (o * (gh * jax.nn.sigmoid(gh))).astype(BF16)


def _gla_pass(p, wl, bl, nlat, reverse, o_f=None, out_g=None):
    blk = lambda s: _walk_block(s, nlat, reverse)
    in_specs = [pl.BlockSpec((TB, 256), lambda s: (blk(s), 8)),
                pl.BlockSpec((TB, 256), lambda s: (blk(s), 9)),
                pl.BlockSpec((TB, DG), lambda s: (blk(s), 5)),
                pl.BlockSpec((TB, 128), lambda s: (blk(s), COL_LR // 128)),
                pl.BlockSpec((128, 256), lambda s: (0, 0)),
                pl.BlockSpec((1, 256), lambda s: (0, 0))]
    args = [p, p, p, p, wl, bl]
    scratch = [pltpu.VMEM((GLA_HEADS * GLA_DV, GLA_HEADS * GLA_DK), F32)]
    if reverse:
        in_specs += [pl.BlockSpec((TB, DG), lambda s: (blk(s), 0)),
                     pl.BlockSpec((TB, DG), lambda s: (blk(s), 6)),
                     pl.BlockSpec((1, GLA_DV), lambda s: (0, 0))]
        args += [o_f, p, out_g]
        scratch = scratch + [pltpu.VMEM((TB, DG), F32)]
    return pl.pallas_call(
        functools.partial(_gla_kernel, reverse=reverse),
        grid=(nlat + 1,),
        in_specs=in_specs,
        out_specs=pl.BlockSpec((TB, DG), lambda s: (blk(s), 0)),
        out_shape=jax.ShapeDtypeStruct(((nlat + 1) * TB, DG), BF16 if reverse else F32),
        scratch_shapes=scratch,
        compiler_params=_cparams(("arbitrary",)),
    )(*args)


def _gla_gate_dense(gate_w, d):
    w = jnp.zeros((128, GLA_HEADS * GLA_DK), F32)
    return w.at[d * GLA_RANK:(d + 1) * GLA_RANK].set(gate_w[d]).astype(BF16)


def _gla_mixer(p, gate_w, gate_b, out_g, nlat=S // TB):
    o_f = _gla_pass(p, _gla_gate_dense(gate_w, 0), gate_b[0:1], nlat, False)
    return _gla_pass(p, _gla_gate_dense(gate_w, 1), gate_b[1:2], nlat, True,
                     o_f=o_f, out_g=out_g[None, :])


FN1 = 64
FN2 = 128
FN2_TILE = 8
FK1_TILE = 8


def _dft_consts():
    def cs(n, rows=None, cols=None):
        a = 2.0 * np.pi * np.outer(np.arange(n if rows is None else rows),
                                   np.arange(n if cols is None else cols)) / n
        return np.cos(a), np.sin(a)

    c1, s1 = cs(FN1)
    m1 = np.concatenate([c1, -s1], axis=0)
    ang = 2.0 * np.pi * np.outer(np.arange(FN1), np.arange(FN2)) / (FN1 * FN2)
    tw_c = np.cos(ang).reshape(FN1, FN2 // FN2_TILE, FN2_TILE).transpose(1, 0, 2)
    tw_s = np.sin(ang).reshape(FN1, FN2 // FN2_TILE, FN2_TILE).transpose(1, 0, 2)
    c2, s2 = cs(FN2)
    m2 = np.block([[c2, s2], [-s2, c2]])
    cc, sc = cs(FNET_CH)
    eye = np.eye(DG // FNET_CH)
    ccb, scb = np.kron(eye, cc), np.kron(eye, sc)
    ch_lat = np.concatenate([ccb, scb], axis=0) / np.sqrt(FN1 * FN2 * FNET_CH)
    ch_ctx = np.concatenate([ccb, -scb], axis=1)
    ct, st = cs(TCX)
    t_ctx = np.concatenate([ct, st], axis=1) / np.sqrt(TCX * FNET_CH)
    bf = lambda a: jnp.asarray(a, F32).astype(BF16)
    return dict(m1=bf(m1), tw_c=jnp.asarray(tw_c, F32), tw_s=jnp.asarray(tw_s, F32),
                m2=bf(m2), ch_lat=bf(ch_lat), ch_ctx=bf(ch_ctx), t_ctx=bf(t_ctx))


def _fnet1_kernel(u_ref, m1_ref, tc_ref, ts_ref, z_ref):
    a = _dot(m1_ref[...], u_ref[...].astype(BF16))
    ar, ai = a[:FN1], a[FN1:]
    for n in range(FN2_TILE):
        sl = slice(n * DG, (n + 1) * DG)
        c = tc_ref[:, n:n + 1]
        sn = ts_ref[:, n:n + 1]
        z_ref[0, :, sl] = (ar[:, sl] * c + ai[:, sl] * sn).astype(BF16)
        z_ref[1, :, sl] = (ai[:, sl] * c - ar[:, sl] * sn).astype(BF16)


def _fnet2_kernel(z_ref, m2_ref, ch_ref, y_ref):
    for i in range(FK1_TILE):
        z = jnp.concatenate([z_ref[0, i], z_ref[1, i]], axis=0)
        y3 = _dot(m2_ref[...], z)
        lhs = jnp.concatenate([y3[:FN2], y3[FN2:]], axis=1).astype(BF16)
        y_ref[:, i * DG:(i + 1) * DG] = _dot(lhs, ch_ref[...]).astype(BF16)


def _fnet_ctx_kernel(u_ref, ch_ref, t_ref, y_in_ref, y_ref):
    del y_in_ref
    v = _dot(u_ref[...].astype(BF16), ch_ref[...])
    rhs = jnp.concatenate([v[:, :DG], v[:, DG:]], axis=0).astype(BF16)
    y_ref[...] = _dot(t_ref[...], rhs).astype(BF16)


def _fnet_mixer(u, consts):
    cols = FN2_TILE * DG
    z = pl.pallas_call(
        _fnet1_kernel,
        grid=(FN2 // FN2_TILE,),
        in_specs=[pl.BlockSpec((FN1, cols), lambda j: (0, j)),
                  pl.BlockSpec((2 * FN1, FN1), lambda j: (0, 0)),
                  pl.BlockSpec((None, FN1, FN2_TILE), lambda j: (j, 0, 0)),
                  pl.BlockSpec((None, FN1, FN2_TILE), lambda j: (j, 0, 0))],
        out_specs=pl.BlockSpec((2, FN1, cols), lambda j: (0, 0, j)),
        out_shape=jax.ShapeDtypeStruct((2, FN1, FN2 * DG), BF16),
        compiler_params=_cparams(("arbitrary",)),
    )(u.reshape(NT // FN2, FN2 * DG), consts['m1'], consts['tw_c'], consts['tw_s'])
    y = pl.pallas_call(
        _fnet2_kernel,
        grid=(FN1 // FK1_TILE,),
        in_specs=[pl.BlockSpec((2, FK1_TILE, FN2, DG), lambda j: (0, j, 0, 0)),
                  pl.BlockSpec((2 * FN2, 2 * FN2), lambda j: (0, 0)),
                  pl.BlockSpec((2 * DG, DG), lambda j: (0, 0))],
        out_specs=pl.BlockSpec((FN2, FK1_TILE * DG), lambda j: (0, j)),
        out_shape=jax.ShapeDtypeStruct((NT // FN1, FN1 * DG), BF16),
        compiler_params=_cparams(("arbitrary",)),
    )(z.reshape(2, FN1, FN2, DG), consts['m2'], consts['ch_lat'])
    y = y.reshape(NT, DG)
    return pl.pallas_call(
        _fnet_ctx_kernel,
        grid=(1,),
        in_specs=[pl.BlockSpec((TCX, DG), lambda j: (S // TCX, 0)),
                  pl.BlockSpec((DG, 2 * DG), lambda j: (0, 0)),
                  pl.BlockSpec((TCX, 2 * TCX), lambda j: (0, 0)),
                  pl.BlockSpec(memory_space=pl.ANY)],
        out_specs=pl.BlockSpec((TCX, DG), lambda j: (S // TCX, 0)),
        out_shape=jax.ShapeDtypeStruct((NT, DG), BF16),
        input_output_aliases={3: 0},
        compiler_params=_cparams(("arbitrary",)),
    )(u, consts['ch_ctx'], consts['t_ctx'], y)


def _rope_tables():
    t = np.arange(S)
    half = 32
    freqs = ROPE_THETA ** (-np.arange(half, dtype=np.float64) / half)
    ang_r = (t // GRID_W)[:, None] * freqs
    ang_c = (t % GRID_W)[:, None] * freqs
    cos = np.concatenate([np.cos(ang_r), np.cos(ang_r), np.cos(ang_c), np.cos(ang_c)], axis=1)
    sin = np.concatenate([-np.sin(ang_r), np.sin(ang_r), -np.sin(ang_c), np.sin(ang_c)], axis=1)
    cos = np.concatenate([cos, np.ones((TCX, ATT_HD))], axis=0)
    sin = np.concatenate([sin, np.zeros((TCX, ATT_HD))], axis=0)
    return jnp.asarray(cos, F32), jnp.asarray(sin, F32)


def _mod_rows(mod, j):
    return jnp.concatenate([mod[0, 3 * j:3 * j + 3], mod[1, 3 * j:3 * j + 3],
                            jnp.zeros((2, D), F32)], axis=0)


def kernel(x, c, ctx, c_ctx, mod_w, mod_b, norm_g, final_g, ffn_w13, ffn_w2, w_in, w_out,
           lru_conv_w, lru_conv_b, lru_gate_w, lru_gate_b, lru_lambda, attn_qk_g,
           gla_gate_w, gla_gate_b, gla_out_g):
    xs = jnp.concatenate([x[0], ctx[0]], axis=0)
    s8 = jnp.concatenate([jax.nn.silu(c), jax.nn.silu(c_ctx)[None, :],
                          jnp.zeros((6, D), F32)], axis=0)
    mods = _mod_all(s8, mod_w, mod_b).reshape(DEPTH, 8, N_MOD, D)
    w_in_p = jnp.concatenate([w_in[:, :, :D_IN_MAIN + 2 * GLA_RANK],
                              jnp.zeros((DEPTH, D, D_P - D_IN_MAIN - 2 * GLA_RANK), F32),
                              w_in[:, :, D_IN_MAIN + 2 * GLA_RANK:]], axis=2)
    cos_t, sin_t = _rope_tables()
    consts = _dft_consts()

    for l in range(DEPTH):
        mod = mods[l]
        g = norm_g[l]
        m0, m1, m2 = _mod_rows(mod, 0), _mod_rows(mod, 1), _mod_rows(mod, 2)
        h = _ffn_a(xs, g[0:1], m0, ffn_w13, l, 0)
        xs = _ffn_b(h, xs, m0, ffn_w2, l, 0)
        p, u = _proj_in(xs, g[1:2], m1, w_in_p[l])
        ya = _lru_mixer(p, lru_conv_w[l], lru_conv_b[l], lru_gate_w[l], lru_gate_b[l], lru_lambda[l])
        yb = _attn_mixer(p, cos_t, sin_t, attn_qk_g[l])
        yc = _gla_mixer(p, gla_gate_w[l], gla_gate_b[l], gla_out_g[l])
        yd = _fnet_mixer(u, consts)
        xs = _proj_out(ya, yb, yc, yd, xs, m1, w_out, l)
        h = _ffn_a(xs, g[2:3], m2, ffn_w13, l, 1)
        xs = _ffn_b(h, xs, m2, ffn_w2, l, 1)
    return _final_norm(xs, final_g[None, :])[None]
```

```python
import functools

import numpy as np
import jax
import jax.numpy as jnp
from jax import lax
from jax.experimental import pallas as pl
from jax.experimental.pallas import tpu as pltpu

F32 = jnp.float32
BF16 = jnp.bfloat16

D = 2048
S = 8192
TCX = 256
NT = S + TCX
DEPTH = 4
GRID_W = 64
EPS = 1e-6
D_FF = 5632
N_MOD = 9
DG = 512
LRU_HEADS = 8
LRU_HD = 64
LRU_C = 8.0
ATT_HD = 128
ATT_KVH = 2
ROPE_THETA = 10000.0
GLA_HEADS = 4
GLA_DK = 64
GLA_DV = 128
GLA_RANK = 16
GLA_TAU = 16.0
GLA_CHUNK = 64
FNET_CH = 128

D_IN_MAIN = 3584
COL_LR = 3584
D_P = 4096
D_IN_PAD = 4608

VMEM_LIMIT = 56 * 1024 * 1024
TM = 1056
TB = 256
ROW_CHUNK = 176


def _cparams(sem):
    return pltpu.CompilerParams(dimension_semantics=sem, vmem_limit_bytes=VMEM_LIMIT)


def _dot(a, b):
    return jnp.dot(a, b, preferred_element_type=F32)


def _dot_nt(a, b):
    return lax.dot_general(a, b, (((1,), (1,)), ((), ())), preferred_element_type=F32)


def _dot_tn(a, b):
    return lax.dot_general(a, b, (((0,), (0,)), ((), ())), preferred_element_type=F32)


def _softplus(x):
    return jnp.maximum(x, 0.0) + jnp.log(1.0 + jnp.exp(-jnp.abs(x)))


def _mod_kernel(s_ref, w_ref, b_ref, o_ref):
    o_ref[...] = _dot(s_ref[...].astype(BF16), w_ref[...].astype(BF16)) + b_ref[...]


def _mod_all(s8, mod_w, mod_b):
    tn = 1024
    return pl.pallas_call(
        _mod_kernel,
        grid=(DEPTH, N_MOD * D // tn),
        in_specs=[pl.BlockSpec((8, D), lambda l, j: (0, 0)),
                  pl.BlockSpec((None, D, tn), lambda l, j: (l, 0, j)),
                  pl.BlockSpec((None, 1, tn), lambda l, j: (l, 0, j))],
        out_specs=pl.BlockSpec((None, 8, tn), lambda l, j: (l, 0, j)),
        out_shape=jax.ShapeDtypeStruct((DEPTH, 8, N_MOD * D), F32),
        compiler_params=_cparams(("arbitrary", "arbitrary")),
    )(s8, mod_w, mod_b.reshape(DEPTH, 1, N_MOD * D))


def _row_select(m, row0, tm, lat_row, ctx_row):
    rows = row0 + lax.broadcasted_iota(jnp.int32, (tm, 1), 0)
    return jnp.where(rows >= S, m[ctx_row:ctx_row + 1, :], m[lat_row:lat_row + 1, :])


LANES = 128


def _norm_mod_store(x_ref, g_ref, m_ref, xn_ref, inv_ref, row0, tm):
    ones = jnp.ones((D, LANES), BF16)

    def body(c, carry):
        rows = pl.ds(pl.multiple_of(c * ROW_CHUNK, 16), ROW_CHUNK)
        x = x_ref[rows, :]
        ss = _dot((x * x).astype(BF16), ones)
        inv_ref[rows, :] = lax.rsqrt(ss * (1.0 / D) + EPS)
        return carry

    lax.fori_loop(0, tm // ROW_CHUNK, body, 0)

    is_ctx = (row0 + lax.broadcasted_iota(jnp.int32, (tm, 1), 0)) >= S
    inv = inv_ref[...]
    for c in range(D // LANES):
        cs = slice(c * LANES, (c + 1) * LANES)
        g = g_ref[:, cs]
        gain = jnp.where(is_ctx, g * (1.0 + m_ref[4:5, cs]), g * (1.0 + m_ref[1:2, cs]))
        shift = jnp.where(is_ctx, m_ref[3:4, cs], m_ref[0:1, cs])
        xn_ref[:, cs] = (x_ref[:, cs] * inv * gain + shift).astype(BF16)


def _ffn_a_kernel(x_ref, g_ref, m_ref, wg_ref, wu_ref, h_ref, xn_ref, inv_ref, *, tm):
    @pl.when(pl.program_id(1) == 0)
    def _():
        _norm_mod_store(x_ref, g_ref, m_ref, xn_ref, inv_ref, pl.program_id(0) * tm, tm)

    xn = xn_ref[...]
    gate = _dot(xn, wg_ref[...].astype(BF16))
    up = _dot(xn, wu_ref[...].astype(BF16))
    h_ref[...] = (gate * jax.nn.sigmoid(gate) * up).astype(BF16)


def _ffn_a(x, g, m, ffn_w13, l, a):
    tm, tf = TM, 512
    nf = D_FF // tf
    return pl.pallas_call(
        functools.partial(_ffn_a_kernel, tm=tm),
        grid=(NT // tm, nf),
        in_specs=[pl.BlockSpec((tm, D), lambda i, j: (i, 0)),
                  pl.BlockSpec((1, D), lambda i, j: (0, 0)),
                  pl.BlockSpec((8, D), lambda i, j: (0, 0)),
                  pl.BlockSpec((None, None, D, tf), lambda i, j: (l, a, 0, j)),
                  pl.BlockSpec((None, None, D, tf), lambda i, j: (l, a, 0, j + nf))],
        out_specs=pl.BlockSpec((tm, tf), lambda i, j: (i, j)),
        out_shape=jax.ShapeDtypeStruct((NT, D_FF), BF16),
        scratch_shapes=[pltpu.VMEM((tm, D), BF16), pltpu.VMEM((tm, LANES), F32)],
        compiler_params=_cparams(("arbitrary", "arbitrary")),
    )(x, g, m, ffn_w13, ffn_w13)


def _ffn_b_kernel(h_ref, w_ref, x_ref, m_ref, o_ref, *, tm, coef):
    acc = _dot(h_ref[...], w_ref[...].astype(BF16))
    gate = _row_select(m_ref[...], pl.program_id(0) * tm, tm, 2, 5)
    o_ref[...] = x_ref[...] + (coef * gate) * acc


def _ffn_b(h, x, m, ffn_w2, l, a):
    tm, tn = TM, 256
    return pl.pallas_call(
        functools.partial(_ffn_b_kernel, tm=tm, coef=0.5),
        grid=(NT // tm, D // tn),
        in_specs=[pl.BlockSpec((tm, D_FF), lambda i, n: (i, 0)),
                  pl.BlockSpec((None, None, D_FF, tn), lambda i, n: (l, a, 0, n)),
                  pl.BlockSpec((tm, tn), lambda i, n: (i, n)),
                  pl.BlockSpec((8, tn), lambda i, n: (0, n))],
        out_specs=pl.BlockSpec((tm, tn), lambda i, n: (i, n)),
        out_shape=jax.ShapeDtypeStruct((NT, D), F32),
        compiler_params=_cparams(("arbitrary", "arbitrary")),
    )(h, ffn_w2, x, m)


def _proj_kernel(x_ref, g_ref, m_ref, w_ref, wt_ref, p_ref, u_ref, xn_ref, inv_ref, *, tm, n_w, n_p):
    j = pl.program_id(1)

    @pl.when(j == 0)
    def _():
        _norm_mod_store(x_ref, g_ref, m_ref, xn_ref, inv_ref, pl.program_id(0) * tm, tm)

    @pl.when(j < n_w)
    def _():
        p_ref[...] = _dot(xn_ref[...], w_ref[...].astype(BF16))

    @pl.when(j == n_w)
    def _():
        p_ref[...] = _dot(xn_ref[...], wt_ref[...].astype(BF16))

    @pl.when(j == n_p)
    def _():
        u_ref[...] = _dot(xn_ref[...], wt_ref[...].astype(BF16))


def _proj_in(x, g, m, w_in, w_tail, l):
    tm, tn = TM, 512
    n_w = D_IN_MAIN // tn
    n_p = D_P // tn
    return pl.pallas_call(
        functools.partial(_proj_kernel, tm=tm, n_w=n_w, n_p=n_p),
        grid=(NT // tm, n_p + 1),
        in_specs=[pl.BlockSpec((tm, D), lambda i, j: (i, 0)),
                  pl.BlockSpec((1, D), lambda i, j: (0, 0)),
                  pl.BlockSpec((8, D), lambda i, j: (0, 0)),
                  pl.BlockSpec((None, D, tn), lambda i, j: (l, 0, jnp.minimum(j, n_w - 1))),
                  pl.BlockSpec((None, D, tn), lambda i, j: (l, 0, jnp.maximum(j - n_w, 0)))],
        out_specs=[pl.BlockSpec((tm, tn), lambda i, j: (i, jnp.minimum(j, n_p - 1))),
                   pl.BlockSpec((tm, DG), lambda i, j: (i, 0))],
        out_shape=[jax.ShapeDtypeStruct((NT, D_P), F32),
                   jax.ShapeDtypeStruct((NT, DG), F32)],
        scratch_shapes=[pltpu.VMEM((tm, D), BF16), pltpu.VMEM((tm, LANES), F32)],
        compiler_params=_cparams(("arbitrary", "arbitrary")),
    )(x, g, m, w_in, w_tail)


def _proj_out_kernel(ya_ref, yb_ref, yc_ref, yd_ref, w_ref, x_ref, m_ref, o_ref, *, tm):
    acc = _dot(ya_ref[...], w_ref[0 * DG:1 * DG, :].astype(BF16))
    acc += _dot(yb_ref[...], w_ref[1 * DG:2 * DG, :].astype(BF16))
    acc += _dot(yc_ref[...], w_ref[2 * DG:3 * DG, :].astype(BF16))
    acc += _dot(yd_ref[...], w_ref[3 * DG:4 * DG, :].astype(BF16))
    gate = _row_select(m_ref[...], pl.program_id(0) * tm, tm, 2, 5)
    o_ref[...] = x_ref[...] + gate * acc


def _proj_out(ya, yb, yc, yd, x, m, w_out, l):
    tm, tn = TM, 512
    yspec = pl.BlockSpec((tm, DG), lambda i, j: (i, 0))
    return pl.pallas_call(
        functools.partial(_proj_out_kernel, tm=tm),
        grid=(NT // tm, D // tn),
        in_specs=[yspec, yspec, yspec, yspec,
                  pl.BlockSpec((None, D, tn), lambda i, j: (l, 0, j)),
                  pl.BlockSpec((tm, tn), lambda i, j: (i, j)),
                  pl.BlockSpec((8, tn), lambda i, j: (0, j))],
        out_specs=pl.BlockSpec((tm, tn), lambda i, j: (i, j)),
        out_shape=jax.ShapeDtypeStruct((NT, D), F32),
        compiler_params=_cparams(("arbitrary", "arbitrary")),
    )(ya, yb, yc, yd, w_out, x, m)


def _final_norm_kernel(x_ref, g_ref, o_ref):
    x = x_ref[...]
    o_ref[...] = x * lax.rsqrt(jnp.mean(x * x, axis=-1, keepdims=True) + EPS) * g_ref[...]


def _final_norm(x, g):
    tm = 1024
    return pl.pallas_call(
        _final_norm_kernel,
        grid=(S // tm,),
        in_specs=[pl.BlockSpec((tm, D), lambda i: (i, 0)),
                  pl.BlockSpec((1, D), lambda i: (0, 0))],
        out_specs=pl.BlockSpec((tm, D), lambda i: (i, 0)),
        out_shape=jax.ShapeDtypeStruct((S, D), F32),
        compiler_params=_cparams(("arbitrary",)),
    )(x, g)


def _rms_rope(t, g, cos, sin_signed, scale):
    y = t * lax.rsqrt(jnp.mean(t * t, axis=-1, keepdims=True) + EPS) * g
    lane = lax.broadcasted_iota(jnp.int32, y.shape, 1)
    partner = jnp.where((lane % 64) < 32, pltpu.roll(y, 96, 1), pltpu.roll(y, 32, 1))
    out = y * cos + partner * sin_signed
    return out * scale if scale != 1.0 else out


LOG2E = 1.4426950408889634


def _attn_prep_kernel(q_ref, k_ref, v_ref, cos_ref, sin_ref, g_ref, qo_ref, kt_ref, vo_ref):
    cos = cos_ref[...]
    sin = sin_ref[...]
    gq = g_ref[0:1, :]
    gk = g_ref[1:2, :]
    q_scale = LOG2E * ATT_HD ** -0.5
    for h in range(4):
        sl = slice(h * ATT_HD, (h + 1) * ATT_HD)
        qo_ref[:, sl] = _rms_rope(q_ref[:, sl], gq, cos, sin, q_scale).astype(BF16)
    for h in range(ATT_KVH):
        sl = slice(h * ATT_HD, (h + 1) * ATT_HD)
        kt_ref[sl, :] = _rms_rope(k_ref[:, sl], gk, cos, sin, 1.0).T.astype(BF16)
    vo_ref[...] = v_ref[...].astype(BF16)


def _attn_prep(p, cos_t, sin_t, qk_g):
    tm = 768
    return pl.pallas_call(
        _attn_prep_kernel,
        grid=(NT // tm,),
        in_specs=[pl.BlockSpec((tm, 512), lambda i: (i, 2)),
                  pl.BlockSpec((tm, 256), lambda i: (i, 6)),
                  pl.BlockSpec((tm, 256), lambda i: (i, 7)),
                  pl.BlockSpec((tm, ATT_HD), lambda i: (i, 0)),
                  pl.BlockSpec((tm, ATT_HD), lambda i: (i, 0)),
                  pl.BlockSpec((2, ATT_HD), lambda i: (0, 0))],
        out_specs=[pl.BlockSpec((tm, 512), lambda i: (i, 0)),
                   pl.BlockSpec((ATT_KVH * ATT_HD, tm), lambda i: (0, i)),
                   pl.BlockSpec((tm, 256), lambda i: (i, 0))],
        out_shape=[jax.ShapeDtypeStruct((NT, 512), BF16),
                   jax.ShapeDtypeStruct((ATT_KVH * ATT_HD, NT), BF16),
                   jax.ShapeDtypeStruct((NT, 256), BF16)],
        compiler_params=_cparams(("arbitrary",)),
    )(p, p, p, cos_t, sin_t, qk_g)


ATT_TQ = 256
ATT_RB = 32
ATT_CB = 384
ATT_KSPLIT = 4352


def _stack_heads(q_ref):
    return jnp.concatenate([q_ref[:, :ATT_HD], q_ref[:, ATT_HD:]], axis=0)


def _store_heads(o_ref, o):
    o_ref[:, :ATT_HD] = o[:ATT_TQ].astype(o_ref.dtype)
    o_ref[:, ATT_HD:] = o[ATT_TQ:].astype(o_ref.dtype)


def _attn_ctx_kernel(q_ref, kt_ref, v_ref, o_ref):
    s = _dot(_stack_heads(q_ref), kt_ref[...])
    e = jnp.exp2(s - jnp.max(s, axis=-1, keepdims=True))
    o = _dot(e.astype(BF16), v_ref[...]) / jnp.sum(e, axis=-1, keepdims=True)
    _store_heads(o_ref, o)


def _attn_ctx(qn, kt, vb):
    blk = S // TCX
    return pl.pallas_call(
        _attn_ctx_kernel,
        grid=(ATT_KVH,),
        in_specs=[pl.BlockSpec((TCX, 2 * ATT_HD), lambda h: (blk, h)),
                  pl.BlockSpec((ATT_HD, TCX), lambda h: (h, blk)),
                  pl.BlockSpec((TCX, ATT_HD), lambda h: (blk, h))],
        out_specs=pl.BlockSpec((TCX, 2 * ATT_HD), lambda h: (blk, h)),
        out_shape=jax.ShapeDtypeStruct((NT, 4 * ATT_HD), BF16),
        compiler_params=_cparams(("arbitrary",)),
    )(qn, kt, vb)


def _attn_full_kernel(q_ref, kt_ref, v_ref, y_in_ref, o_ref, s_ref, p_ref, l_ref):
    del y_in_ref
    q2 = _stack_heads(q_ref)
    s_ref[:, :ATT_KSPLIT] = _dot(q2, kt_ref[:, :ATT_KSPLIT])
    s_ref[:, ATT_KSPLIT:] = _dot(q2, kt_ref[:, ATT_KSPLIT:])
    ncb = NT // ATT_CB

    def body(r, carry):
        rows = pl.ds(pl.multiple_of(r * ATT_RB, ATT_RB), ATT_RB)
        mx = s_ref[rows, 0:ATT_CB]
        for c in range(1, ncb):
            mx = jnp.maximum(mx, s_ref[rows, c * ATT_CB:(c + 1) * ATT_CB])
        m = jnp.max(mx, axis=-1, keepdims=True)
        acc = jnp.zeros((ATT_RB, ATT_CB), F32)
        for c in range(ncb):
            cols = slice(c * ATT_CB, (c + 1) * ATT_CB)
            e = jnp.exp2(s_ref[rows, cols] - m)
            acc = acc + e
            p_ref[rows, cols] = e.astype(BF16)
        l_ref[rows, :] = jnp.sum(acc, axis=-1, keepdims=True)
        return carry

    lax.fori_loop(0, 2 * ATT_TQ // ATT_RB, body, 0)
    o = (_dot(p_ref[:, :ATT_KSPLIT], v_ref[:ATT_KSPLIT, :])
         + _dot(p_ref[:, ATT_KSPLIT:], v_ref[ATT_KSPLIT:, :]))
    _store_heads(o_ref, o / l_ref[...])


def _attn_latent(qn, kt, vb, into):
    return pl.pallas_call(
        _attn_full_kernel,
        grid=(ATT_KVH, S // ATT_TQ),
        in_specs=[pl.BlockSpec((ATT_TQ, 2 * ATT_HD), lambda h, i: (i, h)),
                  pl.BlockSpec((ATT_HD, NT), lambda h, i: (h, 0)),
                  pl.BlockSpec((NT, ATT_HD), lambda h, i: (0, h)),
                  pl.BlockSpec(memory_space=pl.ANY)],
        out_specs=pl.BlockSpec((ATT_TQ, 2 * ATT_HD), lambda h, i: (i, h)),
        out_shape=jax.ShapeDtypeStruct((NT, 4 * ATT_HD), BF16),
        scratch_shapes=[pltpu.VMEM((2 * ATT_TQ, NT), F32),
                        pltpu.VMEM((2 * ATT_TQ, NT), BF16),
                        pltpu.VMEM((2 * ATT_TQ, 1), F32)],
        input_output_aliases={3: 0},
        compiler_params=_cparams(("arbitrary", "arbitrary")),
    )(qn, kt, vb, into)


def _attn_mixer(p, cos_t, sin_t, qk_g):
    qn, kt, vb = _attn_prep(p, cos_t, sin_t, qk_g)
    return _attn_latent(qn, kt, vb, _attn_ctx(qn, kt, vb))


def _walk_block(s, nlat, reverse):
    if reverse:
        return jnp.where(s == 0, nlat, nlat - s)
    return jnp.where(s == 0, nlat, s - 1)


def _seq_edges(s, nlat, reverse):
    if reverse:
        return (s == 0) | (s == nlat), (s == 0) | (s == 1)
    return (s == 0) | (s == 1), (s == 0) | (s == nlat)


def _lru_kernel(xb_ref, xp_ref, xn_ref, cw_ref, cb_ref, wg_ref, bg_ref, lam_ref, *rest,
                nlat, reverse):
    if reverse:
        hf_ref, g_ref, y_ref, xe_ref, carry_ref = rest
    else:
        h_ref, xe_ref, carry_ref = rest
    s = pl.program_id(0)
    starts, ends = _seq_edges(s, nlat, reverse)

    @pl.when(s == 0)
    def _():
        carry_ref[...] = jnp.zeros_like(carry_ref)

    xe_ref[0:8, :] = jnp.where(starts, 0.0, xp_ref[...])
    xe_ref[8:8 + TB, :] = xb_ref[...]
    xe_ref[8 + TB:16 + TB, :] = jnp.where(ends, 0.0, xn_ref[...])
    u = cb_ref[...]
    for j in range(4):
        u = u + cw_ref[j:j + 1, :] * xe_ref[6 + j:6 + j + TB, :]

    gates = _dot(u.astype(BF16), wg_ref[...]) + bg_ref[...]
    r = jax.nn.sigmoid(gates[:, :DG])
    i = jax.nn.sigmoid(gates[:, DG:])
    log_a = (-LRU_C) * r * _softplus(-lam_ref[...])
    a = jnp.exp(log_a)
    b = jnp.sqrt(1.0 - jnp.exp(2.0 * log_a)) * (i * u)

    row = lax.broadcasted_iota(jnp.int32, (TB, 1), 0)
    sh = 1
    while sh < TB:
        if reverse:
            a_s, b_s, valid = pltpu.roll(a, TB - sh, 0), pltpu.roll(b, TB - sh, 0), row < TB - sh
        else:
            a_s, b_s, valid = pltpu.roll(a, sh, 0), pltpu.roll(b, sh, 0), row >= sh
        b = b + jnp.where(valid, a * b_s, 0.0)
        a = jnp.where(valid, a * a_s, a)
        sh *= 2
    h = b + a * carry_ref[0:1, :]
    carry_ref[0:1, :] = h[0:1, :] if reverse else h[TB - 1:TB, :]

    if reverse:
        g = g_ref[...]
        gelu = 0.5 * g * (1.0 + jnp.tanh(0.7978845608028654 * (g + 0.044715 * (g * g * g))))
        y_ref[...] = ((hf_ref[...] + h) * gelu).astype(BF16)
    else:
        h_ref[...] = h


def _lru_pass(p, conv_w, conv_b, wg, bg, lam, nlat, reverse, hf=None):
    blk = lambda s: _walk_block(s, nlat, reverse)
    nrow8 = (nlat + 1) * TB // 8
    in_specs = [pl.BlockSpec((TB, DG), lambda s: (blk(s), 0)),
                pl.BlockSpec((8, DG), lambda s: (jnp.maximum(blk(s) * (TB // 8) - 1, 0), 0)),
                pl.BlockSpec((8, DG), lambda s: (jnp.minimum((blk(s) + 1) * (TB // 8), nrow8 - 1), 0)),
                pl.BlockSpec((4, DG), lambda s: (0, 0)),
                pl.BlockSpec((1, DG), lambda s: (0, 0)),
                pl.BlockSpec((DG, 2 * DG), lambda s: (0, 0)),
                pl.BlockSpec((1, 2 * DG), lambda s: (0, 0)),
                pl.BlockSpec((1, DG), lambda s: (0, 0))]
    args = [p, p, p, conv_w, conv_b, wg, bg, lam]
    if reverse:
        in_specs += [pl.BlockSpec((TB, DG), lambda s: (blk(s), 0)),
                     pl.BlockSpec((TB, DG), lambda s: (blk(s), 1))]
        args += [hf, p]
    return pl.pallas_call(
        functools.partial(_lru_kernel, nlat=nlat, reverse=reverse),
        grid=(nlat + 1,),
        in_specs=in_specs,
        out_specs=pl.BlockSpec((TB, DG), lambda s: (blk(s), 0)),
        out_shape=jax.ShapeDtypeStruct(((nlat + 1) * TB, DG), BF16 if reverse else F32),
        scratch_shapes=[pltpu.VMEM((TB + 16, DG), F32), pltpu.VMEM((8, DG), F32)],
        compiler_params=_cparams(("arbitrary",)),
    )(*args)


def _lru_gate_dense(w):
    eye = jnp.eye(LRU_HEADS, dtype=w.dtype)
    dense = jnp.einsum('ghij,hk->ghikj', w, eye).reshape(2, DG, DG)
    return jnp.concatenate([dense[0], dense[1]], axis=1).astype(BF16)


def _lru_mixer(p, conv_w, conv_b, gate_w, gate_b, lam, nlat=S // TB):
    cb = conv_b[None, :]
    hf = _lru_pass(p, conv_w, cb, _lru_gate_dense(gate_w[0]), gate_b[0].reshape(1, 2 * DG),
                   lam[0:1], nlat, False)
    return _lru_pass(p, conv_w, cb, _lru_gate_dense(gate_w[1]), gate_b[1].reshape(1, 2 * DG),
                     lam[1:2], nlat, True, hf=hf)


def _split3(x):
    hi = x.astype(BF16)
    r1 = x - hi.astype(F32)
    mid = r1.astype(BF16)
    lo = (r1 - mid.astype(F32)).astype(BF16)
    return hi, mid, lo


def _gla_kernel(q_ref, k_ref, v_ref, lr_ref, wl_ref, bl_ref, *rest, reverse):
    if reverse:
        of_ref, g_ref, og_ref, y_ref, st_ref, o_acc = rest
    else:
        o_acc, st_ref = rest
    s = pl.program_id(0)
    nck = TB // GLA_CHUNK
    hw = GLA_HEADS * GLA_DK

    @pl.when(s == 0)
    def _():
        st_ref[...] = jnp.zeros_like(st_ref)

    x = _dot(lr_ref[...].astype(BF16), wl_ref[...]) + bl_ref[...]
    log_a = (jnp.minimum(x, 0.0) - jnp.log(1.0 + jnp.exp(-jnp.abs(x)))) * (1.0 / GLA_TAU)

    ri = lax.broadcasted_iota(jnp.int32, (TB, TB), 0)
    ci = lax.broadcasted_iota(jnp.int32, (TB, TB), 1)
    same = (ri // GLA_CHUNK) == (ci // GLA_CHUNK)
    causal = same & ((ci >= ri) if reverse else (ci <= ri))
    tri = jnp.where(causal, 1.0, 0.0).astype(BF16)
    ones = jnp.where(same, 1.0, 0.0).astype(BF16)
    parts = _split3(log_a)
    bcum = _dot(tri, parts[0]) + _dot(tri, parts[1]) + _dot(tri, parts[2])
    tot = _dot(ones, parts[0]) + _dot(ones, parts[1]) + _dot(ones, parts[2])

    q_t = q_ref[...] * (GLA_DK ** -0.5) * jnp.exp(bcum)
    k_in = (k_ref[...] * jnp.exp(-bcum)).astype(BF16)
    k_out = (k_ref[...] * jnp.exp(tot - bcum)).astype(BF16)
    decay = jnp.exp(tot)
    lane = lax.broadcasted_iota(jnp.int32, (1, hw), 1)

    st = st_ref[...]
    for h in range(GLA_HEADS):
        head = (lane // GLA_DK) == h
        qh = jnp.where(head, q_t, 0.0).astype(BF16)
        vh = v_ref[:, h * GLA_DV:(h + 1) * GLA_DV].astype(BF16)
        att = jnp.where(causal, _dot_nt(qh, k_in), 0.0)
        o_acc[:, h * GLA_DV:(h + 1) * GLA_DV] = _dot(att.astype(BF16), vh)
    for c in (range(nck - 1, -1, -1) if reverse else range(nck)):
        rows = slice(c * GLA_CHUNK, (c + 1) * GLA_CHUNK)
        new_st = []
        for h in range(GLA_HEADS):
            head = (lane // GLA_DK) == h
            st_h = st[h * GLA_DV:(h + 1) * GLA_DV, :]
            qh = jnp.where(head, q_t[rows, :], 0.0).astype(BF16)
            cols = slice(h * GLA_DV, (h + 1) * GLA_DV)
            o_acc[rows, cols] += _dot_nt(qh, st_h.astype(BF16))
            upd = _dot_tn(v_ref[rows, cols].astype(BF16), k_out[rows, :])
            new_st.append(decay[c * GLA_CHUNK:c * GLA_CHUNK + 1, :] * st_h + jnp.where(head, upd, 0.0))
        st = jnp.concatenate(new_st, axis=0)
    st_ref[...] = st

    if reverse:
        g = g_ref[...]
        og = og_ref[...]
        for h in range(GLA_HEADS):
            cols = slice(h * GLA_DV, (h + 1) * GLA_DV)
            o = of_ref[:, cols] + o_acc[:, cols]
            o = o * lax.rsqrt(jnp.mean(o * o, axis=-1, keepdims=True) + EPS) * og
            gh = g[:, cols]
            y_ref[:, cols] = (o * (gh * jax.nn.sigmoid(gh))).astype(BF16)


def _gla_pass(p, wl, bl, nlat, reverse, o_f=None, out_g=None):
    blk = lambda s: _walk_block(s, nlat, reverse)
    in_specs = [pl.BlockSpec((TB, 256), lambda s: (blk(s), 8)),
                pl.BlockSpec((TB, 256), lambda s: (blk(s), 9)),
                pl.BlockSpec((TB, DG), lambda s: (blk(s), 5)),
                pl.BlockSpec((TB, 128), lambda s: (blk(s), COL_LR // 128)),
                pl.BlockSpec((128, 256), lambda s: (0, 0)),
                pl.BlockSpec((1, 256), lambda s: (0, 0))]
    args = [p, p, p, p, wl, bl]
    scratch = [pltpu.VMEM((GLA_HEADS * GLA_DV, GLA_HEADS * GLA_DK), F32)]
    if reverse:
        in_specs += [pl.BlockSpec((TB, DG), lambda s: (blk(s), 0)),
                     pl.BlockSpec((TB, DG), lambda s: (blk(s), 6)),
                     pl.BlockSpec((1, GLA_DV), lambda s: (0, 0))]
        args += [o_f, p, out_g]
        scratch = scratch + [pltpu.VMEM((TB, DG), F32)]
    return pl.pallas_call(
        functools.partial(_gla_kernel, reverse=reverse),
        grid=(nlat + 1,),
        in_specs=in_specs,
        out_specs=pl.BlockSpec((TB, DG), lambda s: (blk(s), 0)),
        out_shape=jax.ShapeDtypeStruct(((nlat + 1) * TB, DG), BF16 if reverse else F32),
        scratch_shapes=scratch,
        compiler_params=_cparams(("arbitrary",)),
    )(*args)


def _gla_gate_dense(gate_w, d):
    w = jnp.zeros((128, GLA_HEADS * GLA_DK), F32)
    return w.at[d * GLA_RANK:(d + 1) * GLA_RANK].set(gate_w[d]).astype(BF16)


def _gla_mixer(p, gate_w, gate_b, out_g, nlat=S // TB):
    o_f = _gla_pass(p, _gla_gate_dense(gate_w, 0), gate_b[0:1], nlat, False)
    return _gla_pass(p, _gla_gate_dense(gate_w, 1), gate_b[1:2], nlat, True,
                     o_f=o_f, out_g=out_g[None, :])


FN1 = 64
FN2 = 128
FN2_TILE = 8
FK1_TILE = 8


def _dft_consts():
    def cs(n, rows=None, cols=None):
        a = 2.0 * np.pi * np.outer(np.arange(n if rows is None else rows),
                                   np.arange(n if cols is None else cols)) / n
        return np.cos(a), np.sin(a)

    c1, s1 = cs(FN1)
    m1 = np.concatenate([c1, -s1], axis=0)
    ang = 2.0 * np.pi * np.outer(np.arange(FN1), np.arange(FN2)) / (FN1 * FN2)
    tw_c = np.cos(ang).reshape(FN1, FN2 // FN2_TILE, FN2_TILE).transpose(1, 0, 2)
    tw_s = np.sin(ang).reshape(FN1, FN2 // FN2_TILE, FN2_TILE).transpose(1, 0, 2)
    c2, s2 = cs(FN2)
    m2 = np.block([[c2, s2], [-s2, c2]])
    cc, sc = cs(FNET_CH)
    eye = np.eye(DG // FNET_CH)
    ccb, scb = np.kron(eye, cc), np.kron(eye, sc)
    ch_lat = np.concatenate([ccb, scb], axis=0) / np.sqrt(FN1 * FN2 * FNET_CH)
    ch_ctx = np.concatenate([ccb, -scb], axis=1)
    ct, st = cs(TCX)
    t_ctx = np.concatenate([ct, st], axis=1) / np.sqrt(TCX * FNET_CH)
    bf = lambda a: jnp.asarray(a, F32).astype(BF16)
    return dict(m1=bf(m1), tw_c=jnp.asarray(tw_c, F32), tw_s=jnp.asarray(tw_s, F32),
                m2=bf(m2), ch_lat=bf(ch_lat), ch_ctx=bf(ch_ctx), t_ctx=bf(t_ctx))


def _fnet1_kernel(u_ref, m1_ref, tc_ref, ts_ref, z_ref):
    a = _dot(m1_ref[...], u_ref[...].astype(BF16))
    ar, ai = a[:FN1], a[FN1:]
    for n in range(FN2_TILE):
        sl = slice(n * DG, (n + 1) * DG)
        c = tc_ref[:, n:n + 1]
        sn = ts_ref[:, n:n + 1]
        z_ref[0, :, sl] = (ar[:, sl] * c + ai[:, sl] * sn).astype(BF16)
        z_ref[1, :, sl] = (ai[:, sl] * c - ar[:, sl] * sn).astype(BF16)


def _fnet2_kernel(z_ref, m2_ref, ch_ref, y_ref):
    for i in range(FK1_TILE):
        z = jnp.concatenate([z_ref[0, i], z_ref[1, i]], axis=0)
        y3 = _dot(m2_ref[...], z)
        lhs = jnp.concatenate([y3[:FN2], y3[FN2:]], axis=1).astype(BF16)
        y_ref[:, i * DG:(i + 1) * DG] = _dot(lhs, ch_ref[...]).astype(BF16)


def _fnet_ctx_kernel(u_ref, ch_ref, t_ref, y_in_ref, y_ref):
    del y_in_ref
    v = _dot(u_ref[...].astype(BF16), ch_ref[...])
    rhs = jnp.concatenate([v[:, :DG], v[:, DG:]], axis=0).astype(BF16)
    y_ref[...] = _dot(t_ref[...], rhs).astype(BF16)


def _fnet_mixer(u, consts):
    cols = FN2_TILE * DG
    z = pl.pallas_call(
        _fnet1_kernel,
        grid=(FN2 // FN2_TILE,),
        in_specs=[pl.BlockSpec((FN1, cols), lambda j: (0, j)),
                  pl.BlockSpec((2 * FN1, FN1), lambda j: (0, 0)),
                  pl.BlockSpec((None, FN1, FN2_TILE), lambda j: (j, 0, 0)),
                  pl.BlockSpec((None, FN1, FN2_TILE), lambda j: (j, 0, 0))],
        out_specs=pl.BlockSpec((2, FN1, cols), lambda j: (0, 0, j)),
        out_shape=jax.ShapeDtypeStruct((2, FN1, FN2 * DG), BF16),
        compiler_params=_cparams(("arbitrary",)),
    )(u.reshape(NT // FN2, FN2 * DG), consts['m1'], consts['tw_c'], consts['tw_s'])
    y = pl.pallas_call(
        _fnet2_kernel,
        grid=(FN1 // FK1_TILE,),
        in_specs=[pl.BlockSpec((2, FK1_TILE, FN2, DG), lambda j: (0, j, 0, 0)),
                  pl.BlockSpec((2 * FN2, 2 * FN2), lambda j: (0, 0)),
                  pl.BlockSpec((2 * DG, DG), lambda j: (0, 0))],
        out_specs=pl.BlockSpec((FN2, FK1_TILE * DG), lambda j: (0, j)),
        out_shape=jax.ShapeDtypeStruct((NT // FN1, FN1 * DG), BF16),
        compiler_params=_cparams(("arbitrary",)),
    )(z.reshape(2, FN1, FN2, DG), consts['m2'], consts['ch_lat'])
    y = y.reshape(NT, DG)
    return pl.pallas_call(
        _fnet_ctx_kernel,
        grid=(1,),
        in_specs=[pl.BlockSpec((TCX, DG), lambda j: (S // TCX, 0)),
                  pl.BlockSpec((DG, 2 * DG), lambda j: (0, 0)),
                  pl.BlockSpec((TCX, 2 * TCX), lambda j: (0, 0)),
                  pl.BlockSpec(memory_space=pl.ANY)],
        out_specs=pl.BlockSpec((TCX, DG), lambda j: (S // TCX, 0)),
        out_shape=jax.ShapeDtypeStruct((NT, DG), BF16),
        input_output_aliases={3: 0},
        compiler_params=_cparams(("arbitrary",)),
    )(u, consts['ch_ctx'], consts['t_ctx'], y)


def _rope_tables():
    t = np.arange(S)
    half = 32
    freqs = ROPE_THETA ** (-np.arange(half, dtype=np.float64) / half)
    ang_r = (t // GRID_W)[:, None] * freqs
    ang_c = (t % GRID_W)[:, None] * freqs
    cos = np.concatenate([np.cos(ang_r), np.cos(ang_r), np.cos(ang_c), np.cos(ang_c)], axis=1)
    sin = np.concatenate([-np.sin(ang_r), np.sin(ang_r), -np.sin(ang_c), np.sin(ang_c)], axis=1)
    cos = np.concatenate([cos, np.ones((TCX, ATT_HD))], axis=0)
    sin = np.concatenate([sin, np.zeros((TCX, ATT_HD))], axis=0)
    return jnp.asarray(cos, F32), jnp.asarray(sin, F32)


def _mod_rows(mod, j):
    return jnp.concatenate([mod[0, 3 * j:3 * j + 3], mod[1, 3 * j:3 * j + 3],
                            jnp.zeros((2, D), F32)], axis=0)


def kernel(x, c, ctx, c_ctx, mod_w, mod_b, norm_g, final_g, ffn_w13, ffn_w2, w_in, w_out,
           lru_conv_w, lru_conv_b, lru_gate_w, lru_gate_b, lru_lambda, attn_qk_g,
           gla_gate_w, gla_gate_b, gla_out_g):
    xs = jnp.concatenate([x[0], ctx[0]], axis=0)
    s8 = jnp.concatenate([jax.nn.silu(c), jax.nn.silu(c_ctx)[None, :],
                          jnp.zeros((6, D), F32)], axis=0)
    mods = _mod_all(s8, mod_w, mod_b).reshape(DEPTH, 8, N_MOD, D)
    w_tail = jnp.concatenate([w_in[:, :, D_IN_MAIN:D_IN_MAIN + 2 * GLA_RANK],
                              jnp.zeros((DEPTH, D, D_P - D_IN_MAIN - 2 * GLA_RANK), F32),
                              w_in[:, :, D_IN_MAIN + 2 * GLA_RANK:]], axis=2)
    cos_t, sin_t = _rope_tables()
    consts = _dft_consts()

    for l in range(DEPTH):
        mod = mods[l]
        g = norm_g[l]
        m0, m1, m2 = _mod_rows(mod, 0), _mod_rows(mod, 1), _mod_rows(mod, 2)
        h = _ffn_a(xs, g[0:1], m0, ffn_w13, l, 0)
        xs = _ffn_b(h, xs, m0, ffn_w2, l, 0)
        p, u = _proj_in(xs, g[1:2], m1, w_in, w_tail, l)
        ya = _lru_mixer(p, lru_conv_w[l], lru_conv_b[l], lru_gate_w[l], lru_gate_b[l], lru_lambda[l])
        yb = _attn_mixer(p, cos_t, sin_t, attn_qk_g[l])
        yc = _gla_mixer(p, gla_gate_w[l], gla_gate_b[l], gla_out_g[l])
        yd = _fnet_mixer(u, consts)
        xs = _proj_out(ya, yb, yc, yd, xs, m1, w_out, l)
        h = _ffn_a(xs, g[2:3], m2, ffn_w13, l, 1)
        xs = _ffn_b(h, xs, m2, ffn_w2, l, 1)
    return _final_norm(xs, final_g[None, :])[None]
```

```python
import functools

import numpy as np
import jax
import jax.numpy as jnp
from jax import lax
from jax.experimental import pallas as pl
from jax.experimental.pallas import tpu as pltpu

F32 = jnp.float32
BF16 = jnp.bfloat16

D = 2048
S = 8192
TCX = 256
NT = S + TCX
DEPTH = 4
GRID_W = 64
EPS = 1e-6
D_FF = 5632
N_MOD = 9
DG = 512
LRU_HEADS = 8
LRU_HD = 64
LRU_C = 8.0
ATT_HD = 128
ATT_KVH = 2
ROPE_THETA = 10000.0
GLA_HEADS = 4
GLA_DK = 64
GLA_DV = 128
GLA_RANK = 16
GLA_TAU = 16.0
GLA_CHUNK = 64
FNET_CH = 128

D_IN_MAIN = 3584
COL_LR = 3584
D_P = 4096
D_IN_PAD = 4608

VMEM_LIMIT = 56 * 1024 * 1024
TM = 1056
TB = 256
ROW_CHUNK = 176


def _cparams(sem):
    return pltpu.CompilerParams(dimension_semantics=sem, vmem_limit_bytes=VMEM_LIMIT)


def _dot(a, b):
    return jnp.dot(a, b, preferred_element_type=F32)


def _dot_nt(a, b):
    return lax.dot_general(a, b, (((1,), (1,)), ((), ())), preferred_element_type=F32)


def _dot_tn(a, b):
    return lax.dot_general(a, b, (((0,), (0,)), ((), ())), preferred_element_type=F32)


def _softplus(x):
    return jnp.maximum(x, 0.0) + jnp.log(1.0 + jnp.exp(-jnp.abs(x)))


def _mod_kernel(s_ref, w_ref, b_ref, o_ref):
    o_ref[...] = _dot(s_ref[...].astype(BF16), w_ref[...].astype(BF16)) + b_ref[...]


def _mod_all(s8, mod_w, mod_b):
    tn = 1024
    return pl.pallas_call(
        _mod_kernel,
        grid=(DEPTH, N_MOD * D // tn),
        in_specs=[pl.BlockSpec((8, D), lambda l, j: (0, 0)),
                  pl.BlockSpec((None, D, tn), lambda l, j: (l, 0, j)),
                  pl.BlockSpec((None, 1, tn), lambda l, j: (l, 0, j))],
        out_specs=pl.BlockSpec((None, 8, tn), lambda l, j: (l, 0, j)),
        out_shape=jax.ShapeDtypeStruct((DEPTH, 8, N_MOD * D), F32),
        compiler_params=_cparams(("arbitrary", "arbitrary")),
        name="mod_vectors",
    )(s8, mod_w, mod_b.reshape(DEPTH, 1, N_MOD * D))


def _row_select(m, row0, tm, lat_row, ctx_row):
    rows = row0 + lax.broadcasted_iota(jnp.int32, (tm, 1), 0)
    return jnp.where(rows >= S, m[ctx_row:ctx_row + 1, :], m[lat_row:lat_row + 1, :])


LANES = 128


def _norm_mod_store(x_ref, g_ref, m_ref, xn_ref, inv_ref, row0, tm):
    ones = jnp.ones((D, LANES), BF16)

    def body(c, carry):
        rows = pl.ds(pl.multiple_of(c * ROW_CHUNK, 16), ROW_CHUNK)
        x = x_ref[rows, :]
        ss = _dot((x * x).astype(BF16), ones)
        inv_ref[rows, :] = lax.rsqrt(ss * (1.0 / D) + EPS)
        return carry

    lax.fori_loop(0, tm // ROW_CHUNK, body, 0)

    is_ctx = (row0 + lax.broadcasted_iota(jnp.int32, (tm, 1), 0)) >= S
    inv = inv_ref[...]
    for c in range(D // LANES):
        cs = slice(c * LANES, (c + 1) * LANES)
        g = g_ref[:, cs]
        gain = jnp.where(is_ctx, g * (1.0 + m_ref[4:5, cs]), g * (1.0 + m_ref[1:2, cs]))
        shift = jnp.where(is_ctx, m_ref[3:4, cs], m_ref[0:1, cs])
        xn_ref[:, cs] = (x_ref[:, cs] * inv * gain + shift).astype(BF16)


def _ffn_a_kernel(x_ref, g_ref, m_ref, wg_ref, wu_ref, h_ref, xn_ref, inv_ref, *, tm):
    @pl.when(pl.program_id(1) == 0)
    def _():
        _norm_mod_store(x_ref, g_ref, m_ref, xn_ref, inv_ref, pl.program_id(0) * tm, tm)

    xn = xn_ref[...]
    gate = _dot(xn, wg_ref[...].astype(BF16))
    up = _dot(xn, wu_ref[...].astype(BF16))
    h_ref[...] = (gate * jax.nn.sigmoid(gate) * up).astype(BF16)


def _ffn_a(x, g, m, ffn_w13, l, a):
    tm, tf = TM, 512
    nf = D_FF // tf
    return pl.pallas_call(
        functools.partial(_ffn_a_kernel, tm=tm),
        grid=(NT // tm, nf),
        in_specs=[pl.BlockSpec((tm, D), lambda i, j: (i, 0)),
                  pl.BlockSpec((1, D), lambda i, j: (0, 0)),
                  pl.BlockSpec((8, D), lambda i, j: (0, 0)),
                  pl.BlockSpec((None, None, D, tf), lambda i, j: (l, a, 0, j)),
                  pl.BlockSpec((None, None, D, tf), lambda i, j: (l, a, 0, j + nf))],
        out_specs=pl.BlockSpec((tm, tf), lambda i, j: (i, j)),
        out_shape=jax.ShapeDtypeStruct((NT, D_FF), BF16),
        scratch_shapes=[pltpu.VMEM((tm, D), BF16), pltpu.VMEM((tm, LANES), F32)],
        compiler_params=_cparams(("arbitrary", "arbitrary")),
        name="ffn_a",
    )(x, g, m, ffn_w13, ffn_w13)


def _ffn_b_kernel(h_ref, w_ref, x_ref, m_ref, o_ref, *, tm, coef):
    acc = _dot(h_ref[...], w_ref[...].astype(BF16))
    gate = _row_select(m_ref[...], pl.program_id(0) * tm, tm, 2, 5)
    o_ref[...] = x_ref[...] + (coef * gate) * acc


def _ffn_b(h, x, m, ffn_w2, l, a):
    tm, tn = TM, 256
    return pl.pallas_call(
        functools.partial(_ffn_b_kernel, tm=tm, coef=0.5),
        grid=(NT // tm, D // tn),
        in_specs=[pl.BlockSpec((tm, D_FF), lambda i, n: (i, 0)),
                  pl.BlockSpec((None, None, D_FF, tn), lambda i, n: (l, a, 0, n)),
                  pl.BlockSpec((tm, tn), lambda i, n: (i, n)),
                  pl.BlockSpec((8, tn), lambda i, n: (0, n))],
        out_specs=pl.BlockSpec((tm, tn), lambda i, n: (i, n)),
        out_shape=jax.ShapeDtypeStruct((NT, D), F32),
        compiler_params=_cparams(("arbitrary", "arbitrary")),
        name="ffn_b",
    )(h, ffn_w2, x, m)


def _proj_kernel(x_ref, g_ref, m_ref, w_ref, wt_ref, p_ref, u_ref, xn_ref, inv_ref, *, tm, n_w, n_p):
    j = pl.program_id(1)

    @pl.when(j == 0)
    def _():
        _norm_mod_store(x_ref, g_ref, m_ref, xn_ref, inv_ref, pl.program_id(0) * tm, tm)

    @pl.when(j < n_w)
    def _():
        p_ref[...] = _dot(xn_ref[...], w_ref[...].astype(BF16))

    @pl.when(j == n_w)
    def _():
        p_ref[...] = _dot(xn_ref[...], wt_ref[...].astype(BF16))

    @pl.when(j == n_p)
    def _():
        u_ref[...] = _dot(xn_ref[...], wt_ref[...].astype(BF16))


def _proj_in(x, g, m, w_in, w_tail, l):
    tm, tn = TM, 512
    n_w = D_IN_MAIN // tn
    n_p = D_P // tn
    return pl.pallas_call(
        functools.partial(_proj_kernel, tm=tm, n_w=n_w, n_p=n_p),
        grid=(NT // tm, n_p + 1),
        in_specs=[pl.BlockSpec((tm, D), lambda i, j: (i, 0)),
                  pl.BlockSpec((1, D), lambda i, j: (0, 0)),
                  pl.BlockSpec((8, D), lambda i, j: (0, 0)),
                  pl.BlockSpec((None, D, tn), lambda i, j: (l, 0, jnp.minimum(j, n_w - 1))),
                  pl.BlockSpec((None, D, tn), lambda i, j: (l, 0, jnp.maximum(j - n_w, 0)))],
        out_specs=[pl.BlockSpec((tm, tn), lambda i, j: (i, jnp.minimum(j, n_p - 1))),
                   pl.BlockSpec((tm, DG), lambda i, j: (i, 0))],
        out_shape=[jax.ShapeDtypeStruct((NT, D_P), F32),
                   jax.ShapeDtypeStruct((NT, DG), F32)],
        scratch_shapes=[pltpu.VMEM((tm, D), BF16), pltpu.VMEM((tm, LANES), F32)],
        compiler_params=_cparams(("arbitrary", "arbitrary")),
        name="proj_in",
    )(x, g, m, w_in, w_tail)


def _proj_out_kernel(ya_ref, yb_ref, yc_ref, yd_ref, w_ref, x_ref, m_ref, o_ref, *, tm):
    acc = _dot(ya_ref[...], w_ref[0 * DG:1 * DG, :].astype(BF16))
    acc += _dot(yb_ref[...], w_ref[1 * DG:2 * DG, :].astype(BF16))
    acc += _dot(yc_ref[...], w_ref[2 * DG:3 * DG, :].astype(BF16))
    acc += _dot(yd_ref[...], w_ref[3 * DG:4 * DG, :].astype(BF16))
    gate = _row_select(m_ref[...], pl.program_id(0) * tm, tm, 2, 5)
    o_ref[...] = x_ref[...] + gate * acc


def _proj_out(ya, yb, yc, yd, x, m, w_out, l):
    tm, tn = TM, 512
    yspec = pl.BlockSpec((tm, DG), lambda i, j: (i, 0))
    return pl.pallas_call(
        functools.partial(_proj_out_kernel, tm=tm),
        grid=(NT // tm, D // tn),
        in_specs=[yspec, yspec, yspec, yspec,
                  pl.BlockSpec((None, D, tn), lambda i, j: (l, 0, j)),
                  pl.BlockSpec((tm, tn), lambda i, j: (i, j)),
                  pl.BlockSpec((8, tn), lambda i, j: (0, j))],
        out_specs=pl.BlockSpec((tm, tn), lambda i, j: (i, j)),
        out_shape=jax.ShapeDtypeStruct((NT, D), F32),
        compiler_params=_cparams(("arbitrary", "arbitrary")),
        name="proj_out",
    )(ya, yb, yc, yd, w_out, x, m)


def _final_norm_kernel(x_ref, g_ref, o_ref):
    x = x_ref[...]
    o_ref[...] = x * lax.rsqrt(jnp.mean(x * x, axis=-1, keepdims=True) + EPS) * g_ref[...]


def _final_norm(x, g):
    tm = 1024
    return pl.pallas_call(
        _final_norm_kernel,
        grid=(S // tm,),
        in_specs=[pl.BlockSpec((tm, D), lambda i: (i, 0)),
                  pl.BlockSpec((1, D), lambda i: (0, 0))],
        out_specs=pl.BlockSpec((tm, D), lambda i: (i, 0)),
        out_shape=jax.ShapeDtypeStruct((S, D), F32),
        compiler_params=_cparams(("arbitrary",)),
        name="final_norm",
    )(x, g)


def _rms_rope(t, g, cos, sin_signed, scale):
    y = t * lax.rsqrt(jnp.mean(t * t, axis=-1, keepdims=True) + EPS) * g
    lane = lax.broadcasted_iota(jnp.int32, y.shape, 1)
    partner = jnp.where((lane % 64) < 32, pltpu.roll(y, 96, 1), pltpu.roll(y, 32, 1))
    out = y * cos + partner * sin_signed
    return out * scale if scale != 1.0 else out


LOG2E = 1.4426950408889634


def _attn_prep_kernel(q_ref, k_ref, v_ref, cos_ref, sin_ref, g_ref, qo_ref, kt_ref, vo_ref):
    cos = cos_ref[...]
    sin = sin_ref[...]
    gq = g_ref[0:1, :]
    gk = g_ref[1:2, :]
    q_scale = LOG2E * ATT_HD ** -0.5
    for h in range(4):
        sl = slice(h * ATT_HD, (h + 1) * ATT_HD)
        qo_ref[:, sl] = _rms_rope(q_ref[:, sl], gq, cos, sin, q_scale).astype(BF16)
    for h in range(ATT_KVH):
        sl = slice(h * ATT_HD, (h + 1) * ATT_HD)
        kt_ref[sl, :] = _rms_rope(k_ref[:, sl], gk, cos, sin, 1.0).T.astype(BF16)
    vo_ref[...] = v_ref[...].astype(BF16)


def _attn_prep(p, cos_t, sin_t, qk_g):
    tm = 768
    return pl.pallas_call(
        _attn_prep_kernel,
        grid=(NT // tm,),
        in_specs=[pl.BlockSpec((tm, 512), lambda i: (i, 2)),
                  pl.BlockSpec((tm, 256), lambda i: (i, 6)),
                  pl.BlockSpec((tm, 256), lambda i: (i, 7)),
                  pl.BlockSpec((tm, ATT_HD), lambda i: (i, 0)),
                  pl.BlockSpec((tm, ATT_HD), lambda i: (i, 0)),
                  pl.BlockSpec((2, ATT_HD), lambda i: (0, 0))],
        out_specs=[pl.BlockSpec((tm, 512), lambda i: (i, 0)),
                   pl.BlockSpec((ATT_KVH * ATT_HD, tm), lambda i: (0, i)),
                   pl.BlockSpec((tm, 256), lambda i: (i, 0))],
        out_shape=[jax.ShapeDtypeStruct((NT, 512), BF16),
                   jax.ShapeDtypeStruct((ATT_KVH * ATT_HD, NT), BF16),
                   jax.ShapeDtypeStruct((NT, 256), BF16)],
        compiler_params=_cparams(("arbitrary",)),
        name="attn_prep",
    )(p, p, p, cos_t, sin_t, qk_g)


ATT_TQ = 256
ATT_RB = 32
ATT_CB = 384
ATT_UNROLL = 2


def _stack_heads(q_ref):
    return jnp.concatenate([q_ref[:, :ATT_HD], q_ref[:, ATT_HD:]], axis=0)


def _store_heads(o_ref, o):
    o_ref[:, :ATT_HD] = o[:ATT_TQ].astype(o_ref.dtype)
    o_ref[:, ATT_HD:] = o[ATT_TQ:].astype(o_ref.dtype)


def _attn_ctx_kernel(q_ref, kt_ref, v_ref, o_ref):
    s = _dot(_stack_heads(q_ref), kt_ref[...])
    e = jnp.exp2(s - jnp.max(s, axis=-1, keepdims=True))
    o = _dot(e.astype(BF16), v_ref[...]) / jnp.sum(e, axis=-1, keepdims=True)
    _store_heads(o_ref, o)


def _attn_ctx(qn, kt, vb):
    blk = S // TCX
    return pl.pallas_call(
        _attn_ctx_kernel,
        grid=(ATT_KVH,),
        in_specs=[pl.BlockSpec((TCX, 2 * ATT_HD), lambda h: (blk, h)),
                  pl.BlockSpec((ATT_HD, TCX), lambda h: (h, blk)),
                  pl.BlockSpec((TCX, ATT_HD), lambda h: (blk, h))],
        out_specs=pl.BlockSpec((TCX, 2 * ATT_HD), lambda h: (0, h)),
        out_shape=jax.ShapeDtypeStruct((TCX, 4 * ATT_HD), BF16),
        compiler_params=_cparams(("arbitrary",)),
        name="attn_ctx",
    )(qn, kt, vb)


ATT_TILES = ATT_KVH * (S // ATT_TQ)


def _attn_full_kernel(q_ref, kt_ref, v_ref, o_ref, s_ref, p_ref, l_ref):
    @pl.when(pl.program_id(0) == 0)
    def _():
        p_ref[...] = jnp.zeros_like(p_ref)
        l_ref[...] = jnp.ones_like(l_ref)

    _store_heads(o_ref, _dot(p_ref[...], v_ref[...]) / l_ref[...])
    s_ref[...] = _dot(_stack_heads(q_ref), kt_ref[...])
    ncb = NT // ATT_CB

    def body(r, carry):
        rows = pl.ds(pl.multiple_of(r * ATT_RB, ATT_RB), ATT_RB)
        mx = s_ref[rows, 0:ATT_CB]
        for c in range(1, ncb):
            mx = jnp.maximum(mx, s_ref[rows, c * ATT_CB:(c + 1) * ATT_CB])
        m = jnp.max(mx, axis=-1, keepdims=True)
        acc = jnp.zeros((ATT_RB, ATT_CB), F32)
        for c in range(ncb):
            cols = slice(c * ATT_CB, (c + 1) * ATT_CB)
            e = jnp.exp2(s_ref[rows, cols] - m)
            acc = acc + e
            p_ref[rows, cols] = e.astype(BF16)
        l_ref[rows, :] = jnp.sum(acc, axis=-1, keepdims=True)
        return carry

    lax.fori_loop(0, 2 * ATT_TQ // ATT_RB, body, 0, unroll=ATT_UNROLL)


def _attn_latent(qn, kt, vb):
    nq = S // ATT_TQ
    cur = lambda t: jnp.minimum(t, ATT_TILES - 1)
    prev = lambda t: jnp.maximum(t - 1, 0)
    return pl.pallas_call(
        _attn_full_kernel,
        grid=(ATT_TILES + 1,),
        in_specs=[pl.BlockSpec((ATT_TQ, 2 * ATT_HD), lambda t: (cur(t) % nq, cur(t) // nq)),
                  pl.BlockSpec((ATT_HD, NT), lambda t: (cur(t) // nq, 0)),
                  pl.BlockSpec((NT, ATT_HD), lambda t: (0, prev(t) // nq))],
        out_specs=pl.BlockSpec((ATT_TQ, 2 * ATT_HD), lambda t: (prev(t) % nq, prev(t) // nq)),
        out_shape=jax.ShapeDtypeStruct((S, 4 * ATT_HD), BF16),
        scratch_shapes=[pltpu.VMEM((2 * ATT_TQ, NT), F32),
                        pltpu.VMEM((2 * ATT_TQ, NT), BF16),
                        pltpu.VMEM((2 * ATT_TQ, 1), F32)],
        compiler_params=_cparams(("arbitrary",)),
        name="attn_latent",
    )(qn, kt, vb)


def _attn_mixer(p, cos_t, sin_t, qk_g):
    qn, kt, vb = _attn_prep(p, cos_t, sin_t, qk_g)
    return jnp.concatenate([_attn_latent(qn, kt, vb), _attn_ctx(qn, kt, vb)], axis=0)


def _walk_block(s, nlat, reverse):
    if reverse:
        return jnp.where(s == 0, nlat, nlat - s)
    return jnp.where(s == 0, nlat, s - 1)


def _seq_edges(s, nlat, reverse):
    if reverse:
        return (s == 0) | (s == nlat), (s == 0) | (s == 1)
    return (s == 0) | (s == 1), (s == 0) | (s == nlat)


def _lru_kernel(xb_ref, xp_ref, xn_ref, cw_ref, cb_ref, wg_ref, bg_ref, lam_ref, *rest,
                nlat, reverse):
    if reverse:
        hf_ref, g_ref, y_ref, xe_ref, carry_ref = rest
    else:
        h_ref, xe_ref, carry_ref = rest
    s = pl.program_id(0)
    starts, ends = _seq_edges(s, nlat, reverse)

    @pl.when(s == 0)
    def _():
        carry_ref[...] = jnp.zeros_like(carry_ref)

    xe_ref[0:8, :] = jnp.where(starts, 0.0, xp_ref[...])
    xe_ref[8:8 + TB, :] = xb_ref[...]
    xe_ref[8 + TB:16 + TB, :] = jnp.where(ends, 0.0, xn_ref[...])
    u = cb_ref[...]
    for j in range(4):
        u = u + cw_ref[j:j + 1, :] * xe_ref[6 + j:6 + j + TB, :]

    gates = _dot(u.astype(BF16), wg_ref[...]) + bg_ref[...]
    r = jax.nn.sigmoid(gates[:, :DG])
    i = jax.nn.sigmoid(gates[:, DG:])
    log_a = (-LRU_C) * r * _softplus(-lam_ref[...])
    a = jnp.exp(log_a)
    b = jnp.sqrt(1.0 - jnp.exp(2.0 * log_a)) * (i * u)

    row = lax.broadcasted_iota(jnp.int32, (TB, 1), 0)
    sh = 1
    while sh < TB:
        if reverse:
            a_s, b_s, valid = pltpu.roll(a, TB - sh, 0), pltpu.roll(b, TB - sh, 0), row < TB - sh
        else:
            a_s, b_s, valid = pltpu.roll(a, sh, 0), pltpu.roll(b, sh, 0), row >= sh
        b = b + jnp.where(valid, a * b_s, 0.0)
        a = jnp.where(valid, a * a_s, a)
        sh *= 2
    h = b + a * carry_ref[0:1, :]
    carry_ref[0:1, :] = h[0:1, :] if reverse else h[TB - 1:TB, :]

    if reverse:
        g = g_ref[...]
        gelu = 0.5 * g * (1.0 + jnp.tanh(0.7978845608028654 * (g + 0.044715 * (g * g * g))))
        y_ref[...] = ((hf_ref[...] + h) * gelu).astype(BF16)
    else:
        h_ref[...] = h


def _lru_pass(p, conv_w, conv_b, wg, bg, lam, nlat, reverse, hf=None):
    blk = lambda s: _walk_block(s, nlat, reverse)
    nrow8 = (nlat + 1) * TB // 8
    in_specs = [pl.BlockSpec((TB, DG), lambda s: (blk(s), 0)),
                pl.BlockSpec((8, DG), lambda s: (jnp.maximum(blk(s) * (TB // 8) - 1, 0), 0)),
                pl.BlockSpec((8, DG), lambda s: (jnp.minimum((blk(s) + 1) * (TB // 8), nrow8 - 1), 0)),
                pl.BlockSpec((4, DG), lambda s: (0, 0)),
                pl.BlockSpec((1, DG), lambda s: (0, 0)),
                pl.BlockSpec((DG, 2 * DG), lambda s: (0, 0)),
                pl.BlockSpec((1, 2 * DG), lambda s: (0, 0)),
                pl.BlockSpec((1, DG), lambda s: (0, 0))]
    args = [p, p, p, conv_w, conv_b, wg, bg, lam]
    if reverse:
        in_specs += [pl.BlockSpec((TB, DG), lambda s: (blk(s), 0)),
                     pl.BlockSpec((TB, DG), lambda s: (blk(s), 1))]
        args += [hf, p]
    return pl.pallas_call(
        functools.partial(_lru_kernel, nlat=nlat, reverse=reverse),
        grid=(nlat + 1,),
        in_specs=in_specs,
        out_specs=pl.BlockSpec((TB, DG), lambda s: (blk(s), 0)),
        out_shape=jax.ShapeDtypeStruct(((nlat + 1) * TB, DG), BF16 if reverse else F32),
        scratch_shapes=[pltpu.VMEM((TB + 16, DG), F32), pltpu.VMEM((8, DG), F32)],
        compiler_params=_cparams(("arbitrary",)),
        name="lru_bwd" if reverse else "lru_fwd",
    )(*args)


def _lru_gate_dense(w):
    eye = jnp.eye(LRU_HEADS, dtype=w.dtype)
    dense = jnp.einsum('ghij,hk->ghikj', w, eye).reshape(2, DG, DG)
    return jnp.concatenate([dense[0], dense[1]], axis=1).astype(BF16)


def _lru_mixer(p, conv_w, conv_b, gate_w, gate_b, lam, nlat=S // TB):
    cb = conv_b[None, :]
    hf = _lru_pass(p, conv_w, cb, _lru_gate_dense(gate_w[0]), gate_b[0].reshape(1, 2 * DG),
                   lam[0:1], nlat, False)
    return _lru_pass(p, conv_w, cb, _lru_gate_dense(gate_w[1]), gate_b[1].reshape(1, 2 * DG),
                     lam[1:2], nlat, True, hf=hf)


def _split3(x):
    hi = x.astype(BF16)
    r1 = x - hi.astype(F32)
    mid = r1.astype(BF16)
    lo = (r1 - mid.astype(F32)).astype(BF16)
    return hi, mid, lo


def _gla_kernel(q_ref, k_ref, v_ref, lr_ref, wl_ref, bl_ref, *rest, reverse):
    if reverse:
        of_ref, g_ref, og_ref, y_ref, st_ref, o_acc = rest
    else:
        o_acc, st_ref = rest
    s = pl.program_id(0)
    nck = TB // GLA_CHUNK
    hw = GLA_HEADS * GLA_DK

    @pl.when(s == 0)
    def _():
        st_ref[...] = jnp.zeros_like(st_ref)

    x = _dot(lr_ref[...].astype(BF16), wl_ref[...]) + bl_ref[...]
    log_a = (jnp.minimum(x, 0.0) - jnp.log(1.0 + jnp.exp(-jnp.abs(x)))) * (1.0 / GLA_TAU)

    ri = lax.broadcasted_iota(jnp.int32, (TB, TB), 0)
    ci = lax.broadcasted_iota(jnp.int32, (TB, TB), 1)
    same = (ri // GLA_CHUNK) == (ci // GLA_CHUNK)
    causal = same & ((ci >= ri) if reverse else (ci <= ri))
    tri = jnp.where(causal, 1.0, 0.0).astype(BF16)
    ones = jnp.where(same, 1.0, 0.0).astype(BF16)
    parts = _split3(log_a)
    bcum = _dot(tri, parts[0]) + _dot(tri, parts[1]) + _dot(tri, parts[2])
    tot = _dot(ones, parts[0]) + _dot(ones, parts[1]) + _dot(ones, parts[2])

    q_t = q_ref[...] * (GLA_DK ** -0.5) * jnp.exp(bcum)
    k_in = (k_ref[...] * jnp.exp(-bcum)).astype(BF16)
    k_out = (k_ref[...] * jnp.exp(tot - bcum)).astype(BF16)
    decay = jnp.exp(tot)
    lane = lax.broadcasted_iota(jnp.int32, (1, hw), 1)

    st = st_ref[...]
    for h in range(GLA_HEADS):
        head = (lane // GLA_DK) == h
        qh = jnp.where(head, q_t, 0.0).astype(BF16)
        vh = v_ref[:, h * GLA_DV:(h + 1) * GLA_DV].astype(BF16)
        att = jnp.where(causal, _dot_nt(qh, k_in), 0.0)
        o_acc[:, h * GLA_DV:(h + 1) * GLA_DV] = _dot(att.astype(BF16), vh)
    for c in (range(nck - 1, -1, -1) if reverse else range(nck)):
        rows = slice(c * GLA_CHUNK, (c + 1) * GLA_CHUNK)
        new_st = []
        for h in range(GLA_HEADS):
            head = (lane // GLA_DK) == h
            st_h = st[h * GLA_DV:(h + 1) * GLA_DV, :]
            qh = jnp.where(head, q_t[rows, :], 0.0).astype(BF16)
            cols = slice(h * GLA_DV, (h + 1) * GLA_DV)
            o_acc[rows, cols] += _dot_nt(qh, st_h.astype(BF16))
            upd = _dot_tn(v_ref[rows, cols].astype(BF16), k_out[rows, :])
            new_st.append(decay[c * GLA_CHUNK:c * GLA_CHUNK + 1, :] * st_h + jnp.where(head, upd, 0.0))
        st = jnp.concatenate(new_st, axis=0)
    st_ref[...] = st

    if reverse:
        g = g_ref[...]
        og = og_ref[...]
        for h in range(GLA_HEADS):
            cols = slice(h * GLA_DV, (h + 1) * GLA_DV)
            o = of_ref[:, cols] + o_acc[:, cols]
            o = o * lax.rsqrt(jnp.mean(o * o, axis=-1, keepdims=True) + EPS) * og
            gh = g[:, cols]
            y_ref[:, cols] = (o * (gh * jax.nn.sigmoid(gh))).astype(BF16)


def _gla_pass(p, wl, bl, nlat, reverse, o_f=None, out_g=None):
    blk = lambda s: _walk_block(s, nlat, reverse)
    in_specs = [pl.BlockSpec((TB, 256), lambda s: (blk(s), 8)),
                pl.BlockSpec((TB, 256), lambda s: (blk(s), 9)),
                pl.BlockSpec((TB, DG), lambda s: (blk(s), 5)),
                pl.BlockSpec((TB, 128), lambda s: (blk(s), COL_LR // 128)),
                pl.BlockSpec((128, 256), lambda s: (0, 0)),
                pl.BlockSpec((1, 256), lambda s: (0, 0))]
    args = [p, p, p, p, wl, bl]
    scratch = [pltpu.VMEM((GLA_HEADS * GLA_DV, GLA_HEADS * GLA_DK), F32)]
    if reverse:
        in_specs += [pl.BlockSpec((TB, DG), lambda s: (blk(s), 0)),
                     pl.BlockSpec((TB, DG), lambda s: (blk(s), 6)),
                     pl.BlockSpec((1, GLA_DV), lambda s: (0, 0))]
        args += [o_f, p, out_g]
        scratch = scratch + [pltpu.VMEM((TB, DG), F32)]
    return pl.pallas_call(
        functools.partial(_gla_kernel, reverse=reverse),
        grid=(nlat + 1,),
        in_specs=in_specs,
        out_specs=pl.BlockSpec((TB, DG), lambda s: (blk(s), 0)),
        out_shape=jax.ShapeDtypeStruct(((nlat + 1) * TB, DG), BF16 if reverse else F32),
        scratch_shapes=scratch,
        compiler_params=_cparams(("arbitrary",)),
        name="gla_bwd" if reverse else "gla_fwd",
    )(*args)


def _gla_gate_dense(gate_w, d):
    w = jnp.zeros((128, GLA_HEADS * GLA_DK), F32)
    return w.at[d * GLA_RANK:(d + 1) * GLA_RANK].set(gate_w[d]).astype(BF16)


def _gla_mixer(p, gate_w, gate_b, out_g, nlat=S // TB):
    o_f = _gla_pass(p, _gla_gate_dense(gate_w, 0), gate_b[0:1], nlat, False)
    return _gla_pass(p, _gla_gate_dense(gate_w, 1), gate_b[1:2], nlat, True,
                     o_f=o_f, out_g=out_g[None, :])


FN1 = 64
FN2 = 128
FN2_TILE = 8
FK1_TILE = 8


def _dft_consts():
    def cs(n, rows=None, cols=None):
        a = 2.0 * np.pi * np.outer(np.arange(n if rows is None else rows),
                                   np.arange(n if cols is None else cols)) / n
        return np.cos(a), np.sin(a)

    c1, s1 = cs(FN1)
    m1 = np.concatenate([c1, -s1], axis=0)
    ang = 2.0 * np.pi * np.outer(np.arange(FN1), np.arange(FN2)) / (FN1 * FN2)
    tw_c = np.cos(ang).reshape(FN1, FN2 // FN2_TILE, FN2_TILE).transpose(1, 0, 2)
    tw_s = np.sin(ang).reshape(FN1, FN2 // FN2_TILE, FN2_TILE).transpose(1, 0, 2)
    c2, s2 = cs(FN2)
    m2 = np.block([[c2, s2], [-s2, c2]])
    cc, sc = cs(FNET_CH)
    eye = np.eye(DG // FNET_CH)
    ccb, scb = np.kron(eye, cc), np.kron(eye, sc)
    ch_lat = np.concatenate([ccb, scb], axis=0) / np.sqrt(FN1 * FN2 * FNET_CH)
    ch_ctx = np.concatenate([ccb, -scb], axis=1)
    ct, st = cs(TCX)
    t_ctx = np.concatenate([ct, st], axis=1) / np.sqrt(TCX * FNET_CH)
    bf = lambda a: jnp.asarray(a, F32).astype(BF16)
    return dict(m1=bf(m1), tw_c=jnp.asarray(tw_c, F32), tw_s=jnp.asarray(tw_s, F32),
                m2=bf(m2), ch_lat=bf(ch_lat), ch_ctx=bf(ch_ctx), t_ctx=bf(t_ctx))


def _fnet1_kernel(u_ref, m1_ref, tc_ref, ts_ref, z_ref):
    a = _dot(m1_ref[...], u_ref[...].astype(BF16))
    ar, ai = a[:FN1], a[FN1:]
    for n in range(FN2_TILE):
        sl = slice(n * DG, (n + 1) * DG)
        c = tc_ref[:, n:n + 1]
        sn = ts_ref[:, n:n + 1]
        z_ref[0, :, sl] = (ar[:, sl] * c + ai[:, sl] * sn).astype(BF16)
        z_ref[1, :, sl] = (ai[:, sl] * c - ar[:, sl] * sn).astype(BF16)


def _fnet2_kernel(z_ref, m2_ref, ch_ref, y_ref):
    for i in range(FK1_TILE):
        z = jnp.concatenate([z_ref[0, i], z_ref[1, i]], axis=0)
        y3 = _dot(m2_ref[...], z)
        lhs = jnp.concatenate([y3[:FN2], y3[FN2:]], axis=1).astype(BF16)
        y_ref[:, i * DG:(i + 1) * DG] = _dot(lhs, ch_ref[...]).astype(BF16)


def _fnet_ctx_kernel(u_ref, ch_ref, t_ref, y_ref):
    v =_dot(u_ref[...].astype(BF16), ch_ref[...])
    rhs = jnp.concatenate([v[:, :DG], v[:, DG:]], axis=0).astype(BF16)
    y_ref[...] = _dot(t_ref[...], rhs).astype(BF16)


def _fnet_mixer(u, consts):
    cols = FN2_TILE * DG
    z = pl.pallas_call(
        _fnet1_kernel,
        grid=(FN2 // FN2_TILE,),
        in_specs=[pl.BlockSpec((FN1, cols), lambda j: (0, j)),
                  pl.BlockSpec((2 * FN1, FN1), lambda j: (0, 0)),
                  pl.BlockSpec((None, FN1, FN2_TILE), lambda j: (j, 0, 0)),
                  pl.BlockSpec((None, FN1, FN2_TILE), lambda j: (j, 0, 0))],
        out_specs=pl.BlockSpec((2, FN1, cols), lambda j: (0, 0, j)),
        out_shape=jax.ShapeDtypeStruct((2, FN1, FN2 * DG), BF16),
        compiler_params=_cparams(("arbitrary",)),
        name="fnet_stage1",
    )(u.reshape(NT // FN2, FN2 * DG), consts['m1'], consts['tw_c'], consts['tw_s'])
    y = pl.pallas_call(
        _fnet2_kernel,
        grid=(FN1 // FK1_TILE,),
        in_specs=[pl.BlockSpec((2, FK1_TILE, FN2, DG), lambda j: (0, j, 0, 0)),
                  pl.BlockSpec((2 * FN2, 2 * FN2), lambda j: (0, 0)),
                  pl.BlockSpec((2 * DG, DG), lambda j: (0, 0))],
        out_specs=pl.BlockSpec((FN2, FK1_TILE * DG), lambda j: (0, j)),
        out_shape=jax.ShapeDtypeStruct((FN2, FN1 * DG), BF16),
        compiler_params=_cparams(("arbitrary",)),
        name="fnet_stage2",
    )(z.reshape(2, FN1, FN2, DG), consts['m2'], consts['ch_lat'])
    y_ctx = pl.pallas_call(
        _fnet_ctx_kernel,
        grid=(1,),
        in_specs=[pl.BlockSpec((TCX, DG), lambda j: (S // TCX, 0)),
                  pl.BlockSpec((DG, 2 * DG), lambda j: (0, 0)),
                  pl.BlockSpec((TCX, 2 * TCX), lambda j: (0, 0))],
        out_specs=pl.BlockSpec((TCX, DG), lambda j: (0, 0)),
        out_shape=jax.ShapeDtypeStruct((TCX, DG), BF16),
        compiler_params=_cparams(("arbitrary",)),
        name="fnet_ctx",
    )(u, consts['ch_ctx'], consts['t_ctx'])
    return jnp.concatenate([y.reshape(S, DG), y_ctx], axis=0)


def _rope_tables():
    t = np.arange(S)
    half = 32
    freqs = ROPE_THETA ** (-np.arange(half, dtype=np.float64) / half)
    ang_r = (t // GRID_W)[:, None] * freqs
    ang_c = (t % GRID_W)[:, None] * freqs
    cos = np.concatenate([np.cos(ang_r), np.cos(ang_r), np.cos(ang_c), np.cos(ang_c)], axis=1)
    sin = np.concatenate([-np.sin(ang_r), np.sin(ang_r), -np.sin(ang_c), np.sin(ang_c)], axis=1)
    cos = np.concatenate([cos, np.ones((TCX, ATT_HD))], axis=0)
    sin = np.concatenate([sin, np.zeros((TCX, ATT_HD))], axis=0)
    return jnp.asarray(cos, F32), jnp.asarray(sin, F32)


def _mod_rows(mod, j):
    return jnp.concatenate([mod[0, 3 * j:3 * j + 3], mod[1, 3 * j:3 * j + 3],
                            jnp.zeros((2, D), F32)], axis=0)


def kernel(x, c, ctx, c_ctx, mod_w, mod_b, norm_g, final_g, ffn_w13, ffn_w2, w_in, w_out,
           lru_conv_w, lru_conv_b, lru_gate_w, lru_gate_b, lru_lambda, attn_qk_g,
           gla_gate_w, gla_gate_b, gla_out_g):
    xs = jnp.concatenate([x[0], ctx[0]], axis=0)
    s8 = jnp.concatenate([jax.nn.silu(c), jax.nn.silu(c_ctx)[None, :],
                          jnp.zeros((6, D), F32)], axis=0)
    mods = _mod_all(s8, mod_w, mod_b).reshape(DEPTH, 8, N_MOD, D)
    w_tail = jnp.concatenate([w_in[:, :, D_IN_MAIN:D_IN_MAIN + 2 * GLA_RANK],
                              jnp.zeros((DEPTH, D, D_P - D_IN_MAIN - 2 * GLA_RANK), F32),
                              w_in[:, :, D_IN_MAIN + 2 * GLA_RANK:]], axis=2)
    cos_t, sin_t = _rope_tables()
    consts = _dft_consts()

    for l in range(DEPTH):
        mod = mods[l]
        g = norm_g[l]
        m0, m1, m2 = _mod_rows(mod, 0), _mod_rows(mod, 1), _mod_rows(mod, 2)
        h = _ffn_a(xs, g[0:1], m0, ffn_w13, l, 0)
        xs = _ffn_b(h, xs, m0, ffn_w2, l, 0)
        p, u = _proj_in(xs, g[1:2], m1, w_in, w_tail, l)
        ya = _lru_mixer(p, lru_conv_w[l], lru_conv_b[l], lru_gate_w[l], lru_gate_b[l], lru_lambda[l])
        yb = _attn_mixer(p, cos_t, sin_t, attn_qk_g[l])
        yc = _gla_mixer(p, gla_gate_w[l], gla_gate_b[l], gla_out_g[l])
        yd = _fnet_mixer(u, consts)
        xs = _proj_out(ya, yb, yc, yd, xs, m1, w_out, l)
        h = _ffn_a(xs, g[2:3], m2, ffn_w13, l, 1)
        xs = _ffn_b(h, xs, m2, ffn_w2, l, 1)
    return _final_norm(xs, final_g[None, :])[None]
```

```python
import functools

import numpy as np
import jax
import jax.numpy as jnp
from jax import lax
from jax.experimental import pallas as pl
from jax.experimental.pallas import tpu as pltpu

F32 = jnp.float32
BF16 = jnp.bfloat16

D = 2048
S = 8192
TCX = 256
NT = S + TCX
DEPTH = 4
GRID_W = 64
EPS = 1e-6
D_FF = 5632
N_MOD = 9
DG = 512
LRU_HEADS = 8
LRU_HD = 64
LRU_C = 8.0
ATT_HD = 128
ATT_KVH = 2
ROPE_THETA = 10000.0
GLA_HEADS = 4
GLA_DK = 64
GLA_DV = 128
GLA_RANK = 16
GLA_TAU = 16.0
GLA_CHUNK = 64
FNET_CH = 128

D_IN_MAIN = 3584
COL_LR = 3584
D_P = 4096
D_IN_PAD = 4608

VMEM_LIMIT = 56 * 1024 * 1024
TM = 1056
TB = 256
ROW_CHUNK = 176


def _cparams(sem):
    return pltpu.CompilerParams(dimension_semantics=sem, vmem_limit_bytes=VMEM_LIMIT)


def _dot(a, b):
    return jnp.dot(a, b, preferred_element_type=F32)


def _dot_nt(a, b):
    return lax.dot_general(a, b, (((1,), (1,)), ((), ())), preferred_element_type=F32)


def _dot_tn(a, b):
    return lax.dot_general(a, b, (((0,), (0,)), ((), ())), preferred_element_type=F32)


def _softplus(x):
    return jnp.maximum(x, 0.0) + jnp.log(1.0 + jnp.exp(-jnp.abs(x)))


def _mod_kernel(s_ref, w_ref, b_ref, o_ref):
    o_ref[...] = _dot(s_ref[...].astype(BF16), w_ref[...].astype(BF16)) + b_ref[...]


def _mod_all(s8, mod_w, mod_b):
    tn = 1024
    return pl.pallas_call(
        _mod_kernel,
        grid=(DEPTH, N_MOD * D // tn),
        in_specs=[pl.BlockSpec((8, D), lambda l, j: (0, 0)),
                  pl.BlockSpec((None, D, tn), lambda l, j: (l, 0, j)),
                  pl.BlockSpec((None, 1, tn), lambda l, j: (l, 0, j))],
        out_specs=pl.BlockSpec((None, 8, tn), lambda l, j: (l, 0, j)),
        out_shape=jax.ShapeDtypeStruct((DEPTH, 8, N_MOD * D), F32),
        compiler_params=_cparams(("arbitrary", "arbitrary")),
        name="mod_vectors",
    )(s8, mod_w, mod_b.reshape(DEPTH, 1, N_MOD * D))


def _row_select(m, row0, tm, lat_row, ctx_row):
    rows = row0 + lax.broadcasted_iota(jnp.int32, (tm, 1), 0)
    return jnp.where(rows >= S, m[ctx_row:ctx_row + 1, :], m[lat_row:lat_row + 1, :])


LANES = 128


def _norm_mod_store(x_ref, g_ref, m_ref, xn_ref, inv_ref, row0, tm):
    ones = jnp.ones((D, LANES), BF16)

    def body(c, carry):
        rows = pl.ds(pl.multiple_of(c * ROW_CHUNK, 16), ROW_CHUNK)
        x = x_ref[rows, :]
        ss = _dot((x * x).astype(BF16), ones)
        inv_ref[rows, :] = lax.rsqrt(ss * (1.0 / D) + EPS)
        return carry

    lax.fori_loop(0, tm // ROW_CHUNK, body, 0)

    is_ctx = (row0 + lax.broadcasted_iota(jnp.int32, (tm, 1), 0)) >= S
    inv = inv_ref[...]
    for c in range(D // LANES):
        cs = slice(c * LANES, (c + 1) * LANES)
        g = g_ref[:, cs]
        gain = jnp.where(is_ctx, g * (1.0 + m_ref[4:5, cs]), g * (1.0 + m_ref[1:2, cs]))
        shift = jnp.where(is_ctx, m_ref[3:4, cs], m_ref[0:1, cs])
        xn_ref[:, cs] = (x_ref[:, cs] * inv * gain + shift).astype(BF16)


def _ffn_a_kernel(x_ref, g_ref, m_ref, wg_ref, wu_ref, h_ref, xn_ref, inv_ref, *, tm):
    @pl.when(pl.program_id(1) == 0)
    def _():
        _norm_mod_store(x_ref, g_ref, m_ref, xn_ref, inv_ref, pl.program_id(0) * tm, tm)

    xn = xn_ref[...]
    gate = _dot(xn, wg_ref[...].astype(BF16))
    up = _dot(xn, wu_ref[...].astype(BF16))
    h_ref[...] = (gate * jax.nn.sigmoid(gate) * up).astype(BF16)


def _ffn_a(x, g, m, ffn_w13, l, a):
    tm, tf = TM, 512
    nf = D_FF // tf
    sub = 3 * l + 2 * a
    return pl.pallas_call(
        functools.partial(_ffn_a_kernel, tm=tm),
        grid=(NT // tm, nf),
        in_specs=[pl.BlockSpec((tm, D), lambda i, j: (i, 0)),
                  pl.BlockSpec((None, 1, D), lambda i, j: (sub, 0, 0)),
                  pl.BlockSpec((None, 8, D), lambda i, j: (sub, 0, 0)),
                  pl.BlockSpec((None, None, D, tf), lambda i, j: (l, a, 0, j)),
                  pl.BlockSpec((None, None, D, tf), lambda i, j: (l, a, 0, j + nf))],
        out_specs=pl.BlockSpec((tm, tf), lambda i, j: (i, j)),
        out_shape=jax.ShapeDtypeStruct((NT, D_FF), BF16),
        scratch_shapes=[pltpu.VMEM((tm, D), BF16), pltpu.VMEM((tm, LANES), F32)],
        compiler_params=_cparams(("arbitrary", "arbitrary")),
        name="ffn_a",
    )(x, g, m, ffn_w13, ffn_w13)


def _ffn_b_kernel(h_ref, w_ref, x_ref, m_ref, o_ref, *, tm, coef):
    acc = _dot(h_ref[...], w_ref[...].astype(BF16))
    gate = _row_select(m_ref[...], pl.program_id(0) * tm, tm, 2, 5)
    o_ref[...] = x_ref[...] + (coef * gate) * acc


def _ffn_b(h, x, m, ffn_w2, l, a):
    tm, tn = TM, 256
    sub = 3 * l + 2 * a
    return pl.pallas_call(
        functools.partial(_ffn_b_kernel, tm=tm, coef=0.5),
        grid=(NT // tm, D // tn),
        in_specs=[pl.BlockSpec((tm, D_FF), lambda i, n: (i, 0)),
                  pl.BlockSpec((None, None, D_FF, tn), lambda i, n: (l, a, 0, n)),
                  pl.BlockSpec((tm, tn), lambda i, n: (i, n)),
                  pl.BlockSpec((None, 8, tn), lambda i, n: (sub, 0, n))],
        out_specs=pl.BlockSpec((tm, tn), lambda i, n: (i, n)),
        out_shape=jax.ShapeDtypeStruct((NT, D), F32),
        compiler_params=_cparams(("arbitrary", "arbitrary")),
        name="ffn_b",
    )(h, ffn_w2, x, m)


def _proj_kernel(x_ref, g_ref, m_ref, w_ref, wt_ref, p_ref, u_ref, xn_ref, inv_ref, *, tm, n_w, n_p):
    j = pl.program_id(1)

    @pl.when(j == 0)
    def _():
        _norm_mod_store(x_ref, g_ref, m_ref, xn_ref, inv_ref, pl.program_id(0) * tm, tm)

    @pl.when(j < n_w)
    def _():
        p_ref[...] = _dot(xn_ref[...], w_ref[...].astype(BF16))

    @pl.when(j == n_w)
    def _():
        p_ref[...] = _dot(xn_ref[...], wt_ref[...].astype(BF16))

    @pl.when(j == n_p)
    def _():
        u_ref[...] = _dot(xn_ref[...], wt_ref[...].astype(BF16))


def _proj_in(x, g, m, w_in, w_tail, l):
    tm, tn = TM, 512
    n_w = D_IN_MAIN // tn
    n_p = D_P // tn
    return pl.pallas_call(
        functools.partial(_proj_kernel, tm=tm, n_w=n_w, n_p=n_p),
        grid=(NT // tm, n_p + 1),
        in_specs=[pl.BlockSpec((tm, D), lambda i, j: (i, 0)),
                  pl.BlockSpec((None, 1, D), lambda i, j: (3 * l + 1, 0, 0)),
                  pl.BlockSpec((None, 8, D), lambda i, j: (3 * l + 1, 0, 0)),
                  pl.BlockSpec((None, D, tn), lambda i, j: (l, 0, jnp.minimum(j, n_w - 1))),
                  pl.BlockSpec((None, D, tn), lambda i, j: (l, 0, jnp.maximum(j - n_w, 0)))],
        out_specs=[pl.BlockSpec((tm, tn), lambda i, j: (i, jnp.minimum(j, n_p - 1))),
                   pl.BlockSpec((tm, DG), lambda i, j: (i, 0))],
        out_shape=[jax.ShapeDtypeStruct((NT, D_P), F32),
                   jax.ShapeDtypeStruct((NT, DG), F32)],
        scratch_shapes=[pltpu.VMEM((tm, D), BF16), pltpu.VMEM((tm, LANES), F32)],
        compiler_params=_cparams(("arbitrary", "arbitrary")),
        name="proj_in",
    )(x, g, m, w_in, w_tail)


def _proj_out_kernel(ya_ref, yb_ref, yc_ref, yd_ref, w_ref, x_ref, m_ref, o_ref, *, tm):
    acc = _dot(ya_ref[...], w_ref[0 * DG:1 * DG, :].astype(BF16))
    acc += _dot(yb_ref[...], w_ref[1 * DG:2 * DG, :].astype(BF16))
    acc += _dot(yc_ref[...], w_ref[2 * DG:3 * DG, :].astype(BF16))
    acc += _dot(yd_ref[...], w_ref[3 * DG:4 * DG, :].astype(BF16))
    gate = _row_select(m_ref[...], pl.program_id(0) * tm, tm, 2, 5)
    o_ref[...] = x_ref[...] + gate * acc


def _proj_out(ya, yb, yc, yd, x, m, w_out, l):
    tm, tn = TM, 512
    yspec = pl.BlockSpec((tm, DG), lambda i, j: (i, 0))
    return pl.pallas_call(
        functools.partial(_proj_out_kernel, tm=tm),
        grid=(NT // tm, D // tn),
        in_specs=[yspec, yspec, yspec, yspec,
                  pl.BlockSpec((None, D, tn), lambda i, j: (l, 0, j)),
                  pl.BlockSpec((tm, tn), lambda i, j: (i, j)),
                  pl.BlockSpec((None, 8, tn), lambda i, j: (3 * l + 1, 0, j))],
        out_specs=pl.BlockSpec((tm, tn), lambda i, j: (i, j)),
        out_shape=jax.ShapeDtypeStruct((NT, D), F32),
        compiler_params=_cparams(("arbitrary", "arbitrary")),
        name="proj_out",
    )(ya, yb, yc, yd, w_out, x, m)


def _final_norm_kernel(x_ref, g_ref, o_ref):
    x = x_ref[...]
    o_ref[...] = x * lax.rsqrt(jnp.mean(x * x, axis=-1, keepdims=True) + EPS) * g_ref[...]


def _final_norm(x, g):
    tm = 1024
    return pl.pallas_call(
        _final_norm_kernel,
        grid=(S // tm,),
        in_specs=[pl.BlockSpec((tm, D), lambda i: (i, 0)),
                  pl.BlockSpec((1, D), lambda i: (0, 0))],
        out_specs=pl.BlockSpec((tm, D), lambda i: (i, 0)),
        out_shape=jax.ShapeDtypeStruct((S, D), F32),
        compiler_params=_cparams(("arbitrary",)),
        name="final_norm",
    )(x, g)


def _rms_rope(t, g, cos, sin_signed, scale):
    y = t * lax.rsqrt(jnp.mean(t * t, axis=-1, keepdims=True) + EPS) * g
    lane = lax.broadcasted_iota(jnp.int32, y.shape, 1)
    partner = jnp.where((lane % 64) < 32, pltpu.roll(y, 96, 1), pltpu.roll(y, 32, 1))
    out = y * cos + partner * sin_signed
    return out * scale if scale != 1.0 else out


LOG2E = 1.4426950408889634


def _attn_prep_kernel(q_ref, k_ref, v_ref, cos_ref, sin_ref, g_ref, qo_ref, kt_ref, vo_ref):
    cos = cos_ref[...]
    sin = sin_ref[...]
    gq = g_ref[0:1, :]
    gk = g_ref[1:2, :]
    q_scale = LOG2E * ATT_HD ** -0.5
    for h in range(4):
        sl = slice(h * ATT_HD, (h + 1) * ATT_HD)
        qo_ref[:, sl] = _rms_rope(q_ref[:, sl], gq, cos, sin, q_scale).astype(BF16)
    for h in range(ATT_KVH):
        sl = slice(h * ATT_HD, (h + 1) * ATT_HD)
        kt_ref[sl, :] = _rms_rope(k_ref[:, sl], gk, cos, sin, 1.0).T.astype(BF16)
    ones = jnp.ones((v_ref.shape[0], ATT_HD), BF16)
    for h in range(ATT_KVH):
        vo_ref[:, 2 * h * ATT_HD:(2 * h + 1) * ATT_HD] = v_ref[:, h * ATT_HD:(h + 1) * ATT_HD].astype(BF16)
        vo_ref[:, (2 * h + 1) * ATT_HD:(2 * h + 2) * ATT_HD] = ones


def _attn_prep(p, cos_t, sin_t, qk_g):
    tm = 768
    return pl.pallas_call(
        _attn_prep_kernel,
        grid=(NT // tm,),
        in_specs=[pl.BlockSpec((tm, 512), lambda i: (i, 2)),
                  pl.BlockSpec((tm, 256), lambda i: (i, 6)),
                  pl.BlockSpec((tm, 256), lambda i: (i, 7)),
                  pl.BlockSpec((tm, ATT_HD), lambda i: (i, 0)),
                  pl.BlockSpec((tm, ATT_HD), lambda i: (i, 0)),
                  pl.BlockSpec((2, ATT_HD), lambda i: (0, 0))],
        out_specs=[pl.BlockSpec((tm, 512), lambda i: (i, 0)),
                   pl.BlockSpec((ATT_KVH * ATT_HD, tm), lambda i: (0, i)),
                   pl.BlockSpec((tm, 2 * ATT_KVH * ATT_HD), lambda i: (i, 0))],
        out_shape=[jax.ShapeDtypeStruct((NT, 512), BF16),
                   jax.ShapeDtypeStruct((ATT_KVH * ATT_HD, NT), BF16),
                   jax.ShapeDtypeStruct((NT, 2 * ATT_KVH * ATT_HD), BF16)],
        compiler_params=_cparams(("arbitrary",)),
        name="attn_prep",
    )(p, p, p, cos_t, sin_t, qk_g)


ATT_TQ = 256
ATT_RB = 32
ATT_CB = 384


def _stack_heads(q_ref):
    return jnp.concatenate([q_ref[:, :ATT_HD], q_ref[:, ATT_HD:]], axis=0)


def _store_heads(o_ref, o):
    o_ref[:, :ATT_HD] = o[:ATT_TQ].astype(o_ref.dtype)
    o_ref[:, ATT_HD:] = o[ATT_TQ:].astype(o_ref.dtype)


def _weighted_values(p, v1):
    ov = _dot(p, v1)
    return ov[:, :ATT_HD] / ov[:, ATT_HD:]


def _attn_ctx_kernel(q_ref, kt_ref, v_ref, o_ref):
    s = _dot(_stack_heads(q_ref), kt_ref[...])
    e = jnp.exp2(s - jnp.max(s, axis=-1, keepdims=True))
    _store_heads(o_ref, _weighted_values(e.astype(BF16), v_ref[...]))


def _attn_ctx(qn, kt, vb):
    blk = S // TCX
    return pl.pallas_call(
        _attn_ctx_kernel,
        grid=(ATT_KVH,),
        in_specs=[pl.BlockSpec((TCX, 2 * ATT_HD), lambda h: (blk, h)),
                  pl.BlockSpec((ATT_HD, TCX), lambda h: (h, blk)),
                  pl.BlockSpec((TCX, 2 * ATT_HD), lambda h: (blk, h))],
        out_specs=pl.BlockSpec((TCX, 2 * ATT_HD), lambda h: (0, h)),
        out_shape=jax.ShapeDtypeStruct((TCX, 4 * ATT_HD), BF16),
        compiler_params=_cparams(("arbitrary",)),
        name="attn_ctx",
    )(qn, kt, vb)


def _softmax_rows(s_ref, p_ref, r):
    rows = slice(r * ATT_RB, (r + 1) * ATT_RB)
    ncb = NT // ATT_CB
    mx = s_ref[rows, 0:ATT_CB]
    for c in range(1, ncb):
        mx = jnp.maximum(mx, s_ref[rows, c * ATT_CB:(c + 1) * ATT_CB])
    m = jnp.max(mx, axis=-1, keepdims=True)
    for c in range(ncb):
        cols = slice(c * ATT_CB, (c + 1) * ATT_CB)
        p_ref[rows, cols] = jnp.exp2(s_ref[rows, cols] - m).astype(BF16)


def _attn_full_kernel(q_ref, kt_ref, v_ref, o0_ref, o1_ref, s_ref, pa_ref, pb_ref):
    @pl.when(pl.program_id(0) == 0)
    def _():
        pb_ref[...] = jnp.ones_like(pb_ref)

    def stacked_q(h):
        return jnp.concatenate([q_ref[:, 2 * h * ATT_HD:(2 * h + 1) * ATT_HD],
                                q_ref[:, (2 * h + 1) * ATT_HD:(2 * h + 2) * ATT_HD]], axis=0)

    nrb = 2 * ATT_TQ // ATT_RB
    s_ref[...] = _dot(stacked_q(0), kt_ref[0:ATT_HD, :])
    _store_heads(o1_ref, _weighted_values(pb_ref[...], v_ref[:, 2 * ATT_HD:4 * ATT_HD]))
    for r in range(nrb):
        _softmax_rows(s_ref, pa_ref, r)
    s_ref[...] = _dot(stacked_q(1), kt_ref[ATT_HD:2 * ATT_HD, :])
    _store_heads(o0_ref, _weighted_values(pa_ref[...], v_ref[:, 0:2 * ATT_HD]))
    for r in range(nrb):
        _softmax_rows(s_ref, pb_ref, r)


def _attn_latent(qn, kt, vb):
    nq = S // ATT_TQ
    cur = lambda g: jnp.minimum(g, nq - 1)
    prev = lambda g: jnp.maximum(g - 1, 0)
    resident = dict(pipeline_mode=pl.Buffered(1))
    half = jax.ShapeDtypeStruct((S, 2 * ATT_HD), BF16)
    return pl.pallas_call(
        _attn_full_kernel,
        grid=(nq + 1,),
        in_specs=[pl.BlockSpec((ATT_TQ, 4 * ATT_HD), lambda g: (cur(g), 0)),
                  pl.BlockSpec((ATT_KVH * ATT_HD, NT), lambda g: (0, 0), **resident),
                  pl.BlockSpec((NT, 2 * ATT_KVH * ATT_HD), lambda g: (0, 0), **resident)],
        out_specs=[pl.BlockSpec((ATT_TQ, 2 * ATT_HD), lambda g: (cur(g), 0)),
                   pl.BlockSpec((ATT_TQ, 2 * ATT_HD), lambda g: (prev(g), 0))],
        out_shape=[half, half],
        scratch_shapes=[pltpu.VMEM((2 * ATT_TQ, NT), F32),
                        pltpu.VMEM((2 * ATT_TQ, NT), BF16),
                        pltpu.VMEM((2 * ATT_TQ, NT), BF16)],
        compiler_params=_cparams(("arbitrary",)),
        name="attn_latent",
    )(qn, kt, vb)


def _attn_mixer(p, cos_t, sin_t, qk_g):
    qn, kt, vb = _attn_prep(p, cos_t, sin_t, qk_g)
    y0, y1 = _attn_latent(qn, kt, vb)
    return jnp.concatenate([jnp.concatenate([y0, y1], axis=1), _attn_ctx(qn, kt, vb)], axis=0)


def _walk_block(s, nlat, reverse):
    if reverse:
        return jnp.where(s == 0, nlat, nlat - s)
    return jnp.where(s == 0, nlat, s - 1)


def _seq_edges(s, nlat, reverse):
    if reverse:
        return (s == 0) | (s == nlat), (s == 0) | (s == 1)
    return (s == 0) | (s == 1), (s == 0) | (s == nlat)


def _lru_kernel(xb_ref, xp_ref, xn_ref, cw_ref, cb_ref, wg_ref, bg_ref, lam_ref, *rest,
                nlat, reverse):
    if reverse:
        hf_ref, g_ref, y_ref, xe_ref, carry_ref = rest
    else:
        h_ref, xe_ref, carry_ref = rest
    s = pl.program_id(0)
    starts, ends = _seq_edges(s, nlat, reverse)

    @pl.when(s == 0)
    def _():
        carry_ref[...] = jnp.zeros_like(carry_ref)

    xe_ref[0:8, :] = jnp.where(starts, 0.0, xp_ref[...])
    xe_ref[8:8 + TB, :] = xb_ref[...]
    xe_ref[8 + TB:16 + TB, :] = jnp.where(ends, 0.0, xn_ref[...])
    u = cb_ref[...]
    for j in range(4):
        u = u + cw_ref[j:j + 1, :] * xe_ref[6 + j:6 + j + TB, :]

    gates = _dot(u.astype(BF16), wg_ref[...]) + bg_ref[...]
    r = jax.nn.sigmoid(gates[:, :DG])
    i = jax.nn.sigmoid(gates[:, DG:])
    log_a = (-LRU_C) * r * _softplus(-lam_ref[...])
    a = jnp.exp(log_a)
    b = jnp.sqrt(1.0 - jnp.exp(2.0 * log_a)) * (i * u)

    row8 = lax.broadcasted_iota(jnp.int32, (TB, 1), 0) % 8
    for sh in (1, 2, 4):
        if reverse:
            a_s, b_s, valid = pltpu.roll(a, TB - sh, 0), pltpu.roll(b, TB - sh, 0), row8 < 8 - sh
        else:
            a_s, b_s, valid = pltpu.roll(a, sh, 0), pltpu.roll(b, sh, 0), row8 >= sh
        b = b + jnp.where(valid, a * b_s, 0.0)
        a = jnp.where(valid, a * a_s, a)
    carry = carry_ref[0:1, :]
    groups = [None] * (TB // 8)
    for gi in (range(TB // 8 - 1, -1, -1) if reverse else range(TB // 8)):
        hg = b[8 * gi:8 * gi + 8, :] + a[8 * gi:8 * gi + 8, :] * carry
        groups[gi] = hg
        carry = hg[0:1, :] if reverse else hg[7:8, :]
    h = jnp.concatenate(groups, axis=0)
    carry_ref[0:1, :] = carry

    if reverse:
        g = g_ref[...]
        gelu = 0.5 * g * (1.0 + jnp.tanh(0.7978845608028654 * (g + 0.044715 * (g * g * g))))
        y_ref[...] = ((hf_ref[...] + h) * gelu).astype(BF16)
    else:
        h_ref[...] = h


def _lru_pass(p, conv_w, conv_b, wg, bg, lam, nlat, reverse, hf=None):
    blk = lambda s: _walk_block(s, nlat, reverse)
    nrow8 = (nlat + 1) * TB // 8
    in_specs = [pl.BlockSpec((TB, DG), lambda s: (blk(s), 0)),
                pl.BlockSpec((8, DG), lambda s: (jnp.maximum(blk(s) * (TB // 8) - 1, 0), 0)),
                pl.BlockSpec((8, DG), lambda s: (jnp.minimum((blk(s) + 1) * (TB // 8), nrow8 - 1), 0)),
                pl.BlockSpec((4, DG), lambda s: (0, 0)),
                pl.BlockSpec((1, DG), lambda s: (0, 0)),
                pl.BlockSpec((DG, 2 * DG), lambda s: (0, 0)),
                pl.BlockSpec((1, 2 * DG), lambda s: (0, 0)),
                pl.BlockSpec((1, DG), lambda s: (0, 0))]
    args = [p, p, p, conv_w, conv_b, wg, bg, lam]
    if reverse:
        in_specs += [pl.BlockSpec((TB, DG), lambda s: (blk(s), 0)),
                     pl.BlockSpec((TB, DG), lambda s: (blk(s), 1))]
        args += [hf, p]
    return pl.pallas_call(
        functools.partial(_lru_kernel, nlat=nlat, reverse=reverse),
        grid=(nlat + 1,),
        in_specs=in_specs,
        out_specs=pl.BlockSpec((TB, DG), lambda s: (blk(s), 0)),
        out_shape=jax.ShapeDtypeStruct(((nlat + 1) * TB, DG), BF16 if reverse else F32),
        scratch_shapes=[pltpu.VMEM((TB + 16, DG), F32), pltpu.VMEM((8, DG), F32)],
        compiler_params=_cparams(("arbitrary",)),
        name="lru_bwd" if reverse else "lru_fwd",
    )(*args)


def _lru_gate_dense(w):
    eye = jnp.eye(LRU_HEADS, dtype=w.dtype)
    dense = jnp.einsum('ghij,hk->ghikj', w, eye).reshape(2, DG, DG)
    return jnp.concatenate([dense[0], dense[1]], axis=1).astype(BF16)


def _lru_mixer(p, conv_w, conv_b, gate_w, gate_b, lam, nlat=S // TB):
    cb = conv_b[None, :]
    hf = _lru_pass(p, conv_w, cb, _lru_gate_dense(gate_w[0]), gate_b[0].reshape(1, 2 * DG),
                   lam[0:1], nlat, False)
    return _lru_pass(p, conv_w, cb, _lru_gate_dense(gate_w[1]), gate_b[1].reshape(1, 2 * DG),
                     lam[1:2], nlat, True, hf=hf)


def _split3(x):
    hi = x.astype(BF16)
    r1 = x - hi.astype(F32)
    mid = r1.astype(BF16)
    lo = (r1 - mid.astype(F32)).astype(BF16)
    return hi, mid, lo


def _gla_kernel(q_ref, k_ref, v_ref, lr_ref, wl_ref, bl_ref, *rest, reverse):
    if reverse:
        of_ref, g_ref, og_ref, y_ref, st_ref, o_acc = rest
    else:
        o_acc, st_ref = rest
    s = pl.program_id(0)
    nck = TB // GLA_CHUNK
    hw = GLA_HEADS * GLA_DK

    @pl.when(s == 0)
    def _():
        st_ref[...] = jnp.zeros_like(st_ref)

    x = _dot(lr_ref[...].astype(BF16), wl_ref[...]) + bl_ref[...]
    log_a = (jnp.minimum(x, 0.0) - jnp.log(1.0 + jnp.exp(-jnp.abs(x)))) * (1.0 / GLA_TAU)

    ri = lax.broadcasted_iota(jnp.int32, (TB, TB), 0)
    ci = lax.broadcasted_iota(jnp.int32, (TB, TB), 1)
    same = (ri // GLA_CHUNK) == (ci // GLA_CHUNK)
    causal = same & ((ci >= ri) if reverse else (ci <= ri))
    tri = jnp.where(causal, 1.0, 0.0).astype(BF16)
    ones = jnp.where(same, 1.0, 0.0).astype(BF16)
    parts = _split3(log_a)
    bcum = _dot(tri, parts[0]) + _dot(tri, parts[1]) + _dot(tri, parts[2])
    tot = _dot(ones, parts[0]) + _dot(ones, parts[1]) + _dot(ones, parts[2])

    q_t = q_ref[...] * (GLA_DK ** -0.5) * jnp.exp(bcum)
    k_in = (k_ref[...] * jnp.exp(-bcum)).astype(BF16)
    k_out = (k_ref[...] * jnp.exp(tot - bcum)).astype(BF16)
    decay = jnp.exp(tot)
    lane = lax.broadcasted_iota(jnp.int32, (1, hw), 1)

    st = st_ref[...]
    for h in range(GLA_HEADS):
        head = (lane // GLA_DK) == h
        qh = jnp.where(head, q_t, 0.0).astype(BF16)
        vh = v_ref[:, h * GLA_DV:(h + 1) * GLA_DV].astype(BF16)
        att = jnp.where(causal, _dot_nt(qh, k_in), 0.0)
        o_acc[:, h * GLA_DV:(h + 1) * GLA_DV] = _dot(att.astype(BF16), vh)
    for c in (range(nck - 1, -1, -1) if reverse else range(nck)):
        rows = slice(c * GLA_CHUNK, (c + 1) * GLA_CHUNK)
        new_st = []
        for h in range(GLA_HEADS):
            head = (lane // GLA_DK) == h
            st_h = st[h * GLA_DV:(h + 1) * GLA_DV, :]
            qh = jnp.where(head, q_t[rows, :], 0.0).astype(BF16)
            cols = slice(h * GLA_DV, (h + 1) * GLA_DV)
            o_acc[rows, cols] += _dot_nt(qh, st_h.astype(BF16))
            upd = _dot_tn(v_ref[rows, cols].astype(BF16), k_out[rows, :])
            new_st.append(decay[c * GLA_CHUNK:c * GLA_CHUNK + 1, :] * st_h + jnp.where(head, upd, 0.0))
        st = jnp.concatenate(new_st, axis=0)
    st_ref[...] = st

    if reverse:
        g = g_ref[...]
        og = og_ref[...]
        for h in range(GLA_HEADS):
            cols = slice(h * GLA_DV, (h + 1) * GLA_DV)
            o = of_ref[:, cols] + o_acc[:, cols]
            o = o * lax.rsqrt(jnp.mean(o * o, axis=-1, keepdims=True) + EPS) * og
            gh = g[:, cols]
            y_ref[:, cols] = (o * (gh * jax.nn.sigmoid(gh))).astype(BF16)


def _gla_pass(p, wl, bl, nlat, reverse, o_f=None, out_g=None):
    blk = lambda s: _walk_block(s, nlat, reverse)
    in_specs = [pl.BlockSpec((TB, 256), lambda s: (blk(s), 8)),
                pl.BlockSpec((TB, 256), lambda s: (blk(s), 9)),
                pl.BlockSpec((TB, DG), lambda s: (blk(s), 5)),
                pl.BlockSpec((TB, 128), lambda s: (blk(s), COL_LR // 128)),
                pl.BlockSpec((128, 256), lambda s: (0, 0)),
                pl.BlockSpec((1, 256), lambda s: (0, 0))]
    args = [p, p, p, p, wl, bl]
    scratch = [pltpu.VMEM((GLA_HEADS * GLA_DV, GLA_HEADS * GLA_DK), F32)]
    if reverse:
        in_specs += [pl.BlockSpec((TB, DG), lambda s: (blk(s), 0)),
                     pl.BlockSpec((TB, DG), lambda s: (blk(s), 6)),
                     pl.BlockSpec((1, GLA_DV), lambda s: (0, 0))]
        args += [o_f, p, out_g]
        scratch = scratch + [pltpu.VMEM((TB, DG), F32)]
    return pl.pallas_call(
        functools.partial(_gla_kernel, reverse=reverse),
        grid=(nlat + 1,),
        in_specs=in_specs,
        out_specs=pl.BlockSpec((TB, DG), lambda s: (blk(s), 0)),
        out_shape=jax.ShapeDtypeStruct(((nlat + 1) * TB, DG), BF16 if reverse else F32),
        scratch_shapes=scratch,
        compiler_params=_cparams(("arbitrary",)),
        name="gla_bwd" if reverse else "gla_fwd",
    )(*args)


def _gla_gate_dense(gate_w, d):
    w = jnp.zeros((128, GLA_HEADS * GLA_DK), F32)
    return w.at[d * GLA_RANK:(d + 1) * GLA_RANK].set(gate_w[d]).astype(BF16)


def _gla_mixer(p, gate_w, gate_b, out_g, nlat=S // TB):
    o_f = _gla_pass(p, _gla_gate_dense(gate_w, 0), gate_b[0:1], nlat, False)
    return _gla_pass(p, _gla_gate_dense(gate_w, 1), gate_b[1:2], nlat, True,
                     o_f=o_f, out_g=out_g[None, :])


FN1 = 64
FN2 = 128
FN2_TILE = 8
FK1_TILE = 8


def _dft_consts():
    def cs(n, rows=None, cols=None):
        a = 2.0 * np.pi * np.outer(np.arange(n if rows is None else rows),
                                   np.arange(n if cols is None else cols)) / n
        return np.cos(a), np.sin(a)

    c1, s1 = cs(FN1)
    m1 = np.concatenate([c1, -s1], axis=0)
    ang = 2.0 * np.pi * np.outer(np.arange(FN1), np.arange(FN2)) / (FN1 * FN2)
    tw_c = np.cos(ang).reshape(FN1, FN2 // FN2_TILE, FN2_TILE).transpose(1, 0, 2)
    tw_s = np.sin(ang).reshape(FN1, FN2 // FN2_TILE, FN2_TILE).transpose(1, 0, 2)
    c2, s2 = cs(FN2)
    m2 = np.block([[c2, s2], [-s2, c2]])
    cc, sc = cs(FNET_CH)
    eye = np.eye(DG // FNET_CH)
    ccb, scb = np.kron(eye, cc), np.kron(eye, sc)
    ch_lat = np.concatenate([ccb, scb], axis=0) / np.sqrt(FN1 * FN2 * FNET_CH)
    ch_ctx = np.concatenate([ccb, -scb], axis=1)
    ct, st = cs(TCX)
    t_ctx = np.concatenate([ct, st], axis=1) / np.sqrt(TCX * FNET_CH)
    bf = lambda a: jnp.asarray(a, F32).astype(BF16)
    return dict(m1=bf(m1), tw_c=jnp.asarray(tw_c, F32), tw_s=jnp.asarray(tw_s, F32),
                m2=bf(m2), ch_lat=bf(ch_lat), ch_ctx=bf(ch_ctx), t_ctx=bf(t_ctx))


def _fnet1_kernel(u_ref, m1_ref, tc_ref, ts_ref, z_ref):
    a = _dot(m1_ref[...], u_ref[...].astype(BF16))
    ar, ai = a[:FN1], a[FN1:]
    for n in range(FN2_TILE):
        sl = slice(n * DG, (n + 1) * DG)
        c = tc_ref[:, n:n + 1]
        sn = ts_ref[:, n:n + 1]
        z_ref[0, :, sl] = (ar[:, sl] * c + ai[:, sl] * sn).astype(BF16)
        z_ref[1, :, sl] = (ai[:, sl] * c - ar[:, sl] * sn).astype(BF16)


def _fnet2_kernel(z_ref, m2_ref, ch_ref, y_ref):
    for i in range(FK1_TILE):
        z = jnp.concatenate([z_ref[0, i], z_ref[1, i]], axis=0)
        y3 = _dot(m2_ref[...], z)
        lhs = jnp.concatenate([y3[:FN2], y3[FN2:]], axis=1).astype(BF16)
        y_ref[:, i * DG:(i + 1) * DG] = _dot(lhs, ch_ref[...]).astype(BF16)


def _fnet_ctx_kernel(u_ref, ch_ref, t_ref, y_ref):
    v =_dot(u_ref[...].astype(BF16), ch_ref[...])
    rhs = jnp.concatenate([v[:, :DG], v[:, DG:]], axis=0).astype(BF16)
    y_ref[...] = _dot(t_ref[...], rhs).astype(BF16)


def _fnet_mixer(u, consts):
    cols = FN2_TILE * DG
    z = pl.pallas_call(
        _fnet1_kernel,
        grid=(FN2 // FN2_TILE,),
        in_specs=[pl.BlockSpec((FN1, cols), lambda j: (0, j)),
                  pl.BlockSpec((2 * FN1, FN1), lambda j: (0, 0)),
                  pl.BlockSpec((None, FN1, FN2_TILE), lambda j: (j, 0, 0)),
                  pl.BlockSpec((None, FN1, FN2_TILE), lambda j: (j, 0, 0))],
        out_specs=pl.BlockSpec((2, FN1, cols), lambda j: (0, 0, j)),
        out_shape=jax.ShapeDtypeStruct((2, FN1, FN2 * DG), BF16),
        compiler_params=_cparams(("arbitrary",)),
        name="fnet_stage1",
    )(u.reshape(NT // FN2, FN2 * DG), consts['m1'], consts['tw_c'], consts['tw_s'])
    y = pl.pallas_call(
        _fnet2_kernel,
        grid=(FN1 // FK1_TILE,),
        in_specs=[pl.BlockSpec((2, FK1_TILE, FN2, DG), lambda j: (0, j, 0, 0)),
                  pl.BlockSpec((2 * FN2, 2 * FN2), lambda j: (0, 0)),
                  pl.BlockSpec((2 * DG, DG), lambda j: (0, 0))],
        out_specs=pl.BlockSpec((FN2, FK1_TILE * DG), lambda j: (0, j)),
        out_shape=jax.ShapeDtypeStruct((FN2, FN1 * DG), BF16),
        compiler_params=_cparams(("arbitrary",)),
        name="fnet_stage2",
    )(z.reshape(2, FN1, FN2, DG), consts['m2'], consts['ch_lat'])
    y_ctx = pl.pallas_call(
        _fnet_ctx_kernel,
        grid=(1,),
        in_specs=[pl.BlockSpec((TCX, DG), lambda j: (S // TCX, 0)),
                  pl.BlockSpec((DG, 2 * DG), lambda j: (0, 0)),
                  pl.BlockSpec((TCX, 2 * TCX), lambda j: (0, 0))],
        out_specs=pl.BlockSpec((TCX, DG), lambda j: (0, 0)),
        out_shape=jax.ShapeDtypeStruct((TCX, DG), BF16),
        compiler_params=_cparams(("arbitrary",)),
        name="fnet_ctx",
    )(u, consts['ch_ctx'], consts['t_ctx'])
    return jnp.concatenate([y.reshape(S, DG), y_ctx], axis=0)


def _rope_tables():
    t = np.arange(S)
    half = 32
    freqs = ROPE_THETA ** (-np.arange(half, dtype=np.float64) / half)
    ang_r = (t // GRID_W)[:, None] * freqs
    ang_c = (t % GRID_W)[:, None] * freqs
    cos = np.concatenate([np.cos(ang_r), np.cos(ang_r), np.cos(ang_c), np.cos(ang_c)], axis=1)
    sin = np.concatenate([-np.sin(ang_r), np.sin(ang_r), -np.sin(ang_c), np.sin(ang_c)], axis=1)
    cos = np.concatenate([cos, np.ones((TCX, ATT_HD))], axis=0)
    sin = np.concatenate([sin, np.zeros((TCX, ATT_HD))], axis=0)
    return jnp.asarray(cos, F32), jnp.asarray(sin, F32)


def _mod_rows(mods):
    m = mods[:, 0:2].reshape(DEPTH, 2, 3, 3, D).transpose(0, 2, 1, 3, 4).reshape(3 * DEPTH, 6, D)
    return jnp.pad(m, ((0, 0), (0, 2), (0, 0)))


def kernel(x, c, ctx, c_ctx, mod_w, mod_b, norm_g, final_g, ffn_w13, ffn_w2, w_in, w_out,
           lru_conv_w, lru_conv_b, lru_gate_w, lru_gate_b, lru_lambda, attn_qk_g,
           gla_gate_w, gla_gate_b, gla_out_g):
    xs = jnp.concatenate([x[0], ctx[0]], axis=0)
    s8 = jnp.concatenate([jax.nn.silu(c), jax.nn.silu(c_ctx)[None, :],
                          jnp.zeros((6, D), F32)], axis=0)
    m = _mod_rows(_mod_all(s8, mod_w, mod_b))
    g = norm_g.reshape(3 * DEPTH, 1, D)
    w_tail = jnp.concatenate([w_in[:, :, D_IN_MAIN:D_IN_MAIN + 2 * GLA_RANK],
                              jnp.zeros((DEPTH, D, D_P - D_IN_MAIN - 2 * GLA_RANK), F32),
                              w_in[:, :, D_IN_MAIN + 2 * GLA_RANK:]], axis=2)
    cos_t, sin_t = _rope_tables()
    consts = _dft_consts()

    for l in range(DEPTH):
        h = _ffn_a(xs, g, m, ffn_w13, l, 0)
        xs = _ffn_b(h, xs, m, ffn_w2, l, 0)
        p, u = _proj_in(xs, g, m, w_in, w_tail, l)
        ya = _lru_mixer(p, lru_conv_w[l], lru_conv_b[l], lru_gate_w[l], lru_gate_b[l], lru_lambda[l])
        yb = _attn_mixer(p, cos_t, sin_t, attn_qk_g[l])
        yc = _gla_mixer(p, gla_gate_w[l], gla_gate_b[l], gla_out_g[l])
        yd = _fnet_mixer(u, consts)
        xs = _proj_out(ya, yb, yc, yd, xs, m, w_out, l)
        h = _ffn_a(xs, g, m, ffn_w13, l, 1)
        xs = _ffn_b(h, xs, m, ffn_w2, l, 1)
    return _final_norm(xs, final_g[None, :])[None]
```

```python
import functools

import numpy as np
import jax
import jax.numpy as jnp
from jax import lax
from jax.experimental import pallas as pl
from jax.experimental.pallas import tpu as pltpu

F32 = jnp.float32
BF16 = jnp.bfloat16

D = 2048
S = 8192
TCX = 256
NT = S + TCX
DEPTH = 4
GRID_W = 64
EPS = 1e-6
D_FF = 5632
N_MOD = 9
DG = 512
LRU_HEADS = 8
LRU_HD = 64
LRU_C = 8.0
ATT_HD = 128
ATT_KVH = 2
ROPE_THETA = 10000.0
GLA_HEADS = 4
GLA_DK = 64
GLA_DV = 128
GLA_RANK = 16
GLA_TAU = 16.0
GLA_CHUNK = 64
FNET_CH = 128

D_IN_MAIN = 3584
COL_LR = 3584
D_P = 4096

V7X_VMEM_LIMIT_BYTES = 56 * 1024 * 1024
TM = 1056
TB = 256
ROW_CHUNK = 176


def _cparams(sem):
    return pltpu.CompilerParams(dimension_semantics=sem, vmem_limit_bytes=V7X_VMEM_LIMIT_BYTES)


def _dot(a, b):
    return jnp.dot(a, b, preferred_element_type=F32)


def _dot_nt(a, b):
    return lax.dot_general(a, b, (((1,), (1,)), ((), ())), preferred_element_type=F32)


def _dot_tn(a, b):
    return lax.dot_general(a, b, (((0,), (0,)), ((), ())), preferred_element_type=F32)


def _softplus(x):
    return jnp.maximum(x, 0.0) + jnp.log(1.0 + jnp.exp(-jnp.abs(x)))


def _sigmoid(x):
    return 0.5 * jnp.tanh(0.5 * x) + 0.5


def _mod_kernel(s_ref, w_ref, b_ref, o_ref):
    o_ref[...] = _dot(s_ref[...].astype(BF16), w_ref[...].astype(BF16)) + b_ref[...]


def _mod_all(s8, mod_w, mod_b):
    tn = 1024
    return pl.pallas_call(
        _mod_kernel,
        grid=(DEPTH, N_MOD * D // tn),
        in_specs=[pl.BlockSpec((8, D), lambda l, j: (0, 0)),
                  pl.BlockSpec((None, D, tn), lambda l, j: (l, 0, j)),
                  pl.BlockSpec((None, 1, tn), lambda l, j: (l, 0, j))],
        out_specs=pl.BlockSpec((None, 8, tn), lambda l, j: (l, 0, j)),
        out_shape=jax.ShapeDtypeStruct((DEPTH, 8, N_MOD * D), F32),
        compiler_params=_cparams(("arbitrary", "arbitrary")),
        name="mod_vectors",
    )(s8, mod_w, mod_b.reshape(DEPTH, 1, N_MOD * D))


def _row_select(m, row0, tm, lat_row, ctx_row):
    rows = row0 + lax.broadcasted_iota(jnp.int32, (tm, 1), 0)
    return jnp.where(rows >= S, m[ctx_row:ctx_row + 1, :], m[lat_row:lat_row + 1, :])


LANES = 128


def _norm_mod_store(x_ref, g_ref, m_ref, xn_ref, inv_ref, row0, tm):
    ones = jnp.ones((D, LANES), BF16)

    def body(c, carry):
        rows = pl.ds(pl.multiple_of(c * ROW_CHUNK, 16), ROW_CHUNK)
        x = x_ref[rows, :]
        ss = _dot((x * x).astype(BF16), ones)
        inv_ref[rows, :] = lax.rsqrt(ss * (1.0 / D) + EPS)
        return carry

    lax.fori_loop(0, tm // ROW_CHUNK, body, 0)

    is_ctx = (row0 + lax.broadcasted_iota(jnp.int32, (tm, 1), 0)) >= S
    inv = inv_ref[...]
    for c in range(D // LANES):
        cs = slice(c * LANES, (c + 1) * LANES)
        g = g_ref[:, cs]
        gain = jnp.where(is_ctx, g * (1.0 + m_ref[4:5, cs]), g * (1.0 + m_ref[1:2, cs]))
        shift = jnp.where(is_ctx, m_ref[3:4, cs], m_ref[0:1, cs])
        xn_ref[:, cs] = (x_ref[:, cs] * inv * gain + shift).astype(BF16)


def _ffn_a_kernel(x_ref, g_ref, m_ref, wg_ref, wu_ref, h_ref, xn_ref, inv_ref, *, tm):
    @pl.when(pl.program_id(1) == 0)
    def _():
        _norm_mod_store(x_ref, g_ref, m_ref, xn_ref, inv_ref, pl.program_id(0) * tm, tm)

    xn = xn_ref[...]
    gate = _dot(xn, wg_ref[...].astype(BF16))
    up = _dot(xn, wu_ref[...].astype(BF16))
    h_ref[...] = (gate * _sigmoid(gate) * up).astype(BF16)


def _ffn_a(x, g, m, ffn_w13, l, a):
    tm, tf = TM, 512
    nf = D_FF // tf
    sub = 3 * l + 2 * a
    return pl.pallas_call(
        functools.partial(_ffn_a_kernel, tm=tm),
        grid=(NT // tm, nf),
        in_specs=[pl.BlockSpec((tm, D), lambda i, j: (i, 0)),
                  pl.BlockSpec((None, 1, D), lambda i, j: (sub, 0, 0)),
                  pl.BlockSpec((None, 8, D), lambda i, j: (sub, 0, 0)),
                  pl.BlockSpec((None, None, D, tf), lambda i, j: (l, a, 0, j)),
                  pl.BlockSpec((None, None, D, tf), lambda i, j: (l, a, 0, j + nf))],
        out_specs=pl.BlockSpec((tm, tf), lambda i, j: (i, j)),
        out_shape=jax.ShapeDtypeStruct((NT, D_FF), BF16),
        scratch_shapes=[pltpu.VMEM((tm, D), BF16), pltpu.VMEM((tm, LANES), F32)],
        compiler_params=_cparams(("arbitrary", "arbitrary")),
        name="ffn_a",
    )(x, g, m, ffn_w13, ffn_w13)


def _ffn_b_kernel(h_ref, w_ref, x_ref, m_ref, o_ref, *, tm, coef):
    acc = _dot(h_ref[...], w_ref[...].astype(BF16))
    gate = _row_select(m_ref[...], pl.program_id(0) * tm, tm, 2, 5)
    o_ref[...] = x_ref[...] + (coef * gate) * acc


def _ffn_b(h, x, m, ffn_w2, l, a):
    tm, tn = TM, 256
    sub = 3 * l + 2 * a
    return pl.pallas_call(
        functools.partial(_ffn_b_kernel, tm=tm, coef=0.5),
        grid=(NT // tm, D // tn),
        in_specs=[pl.BlockSpec((tm, D_FF), lambda i, n: (i, 0)),
                  pl.BlockSpec((None, None, D_FF, tn), lambda i, n: (l, a, 0, n)),
                  pl.BlockSpec((tm, tn), lambda i, n: (i, n)),
                  pl.BlockSpec((None, 8, tn), lambda i, n: (sub, 0, n))],
        out_specs=pl.BlockSpec((tm, tn), lambda i, n: (i, n)),
        out_shape=jax.ShapeDtypeStruct((NT, D), F32),
        compiler_params=_cparams(("arbitrary", "arbitrary")),
        name="ffn_b",
    )(h, ffn_w2, x, m)


def _proj_kernel(x_ref, g_ref, m_ref, w_ref, wt_ref, p_ref, u_ref, xn_ref, inv_ref, *, tm, n_w, n_p):
    j = pl.program_id(1)

    @pl.when(j == 0)
    def _():
        _norm_mod_store(x_ref, g_ref, m_ref, xn_ref, inv_ref, pl.program_id(0) * tm, tm)

    @pl.when(j < n_w)
    def _():
        p_ref[...] = _dot(xn_ref[...], w_ref[...].astype(BF16))

    @pl.when(j == n_w)
    def _():
        p_ref[...] = _dot(xn_ref[...], wt_ref[...].astype(BF16))

    @pl.when(j == n_p)
    def _():
        u_ref[...] = _dot(xn_ref[...], wt_ref[...].astype(BF16))


def _proj_in(x, g, m, w_in, w_tail, l):
    tm, tn = TM, 512
    n_w = D_IN_MAIN // tn
    n_p = D_P // tn
    return pl.pallas_call(
        functools.partial(_proj_kernel, tm=tm, n_w=n_w, n_p=n_p),
        grid=(NT // tm, n_p + 1),
        in_specs=[pl.BlockSpec((tm, D), lambda i, j: (i, 0)),
                  pl.BlockSpec((None, 1, D), lambda i, j: (3 * l + 1, 0, 0)),
                  pl.BlockSpec((None, 8, D), lambda i, j: (3 * l + 1, 0, 0)),
                  pl.BlockSpec((None, D, tn), lambda i, j: (l, 0, jnp.minimum(j, n_w - 1))),
                  pl.BlockSpec((None, D, tn), lambda i, j: (l, 0, jnp.maximum(j - n_w, 0)))],
        out_specs=[pl.BlockSpec((tm, tn), lambda i, j: (i, jnp.minimum(j, n_p - 1))),
                   pl.BlockSpec((tm, DG), lambda i, j: (i, 0))],
        out_shape=[jax.ShapeDtypeStruct((NT, D_P), F32),
                   jax.ShapeDtypeStruct((NT, DG), F32)],
        scratch_shapes=[pltpu.VMEM((tm, D), BF16), pltpu.VMEM((tm, LANES), F32)],
        compiler_params=_cparams(("arbitrary", "arbitrary")),
        name="proj_in",
    )(x, g, m, w_in, w_tail)


def _proj_out_kernel(ya_ref, yb_ref, yc_ref, yd_ref, w_ref, x_ref, m_ref, o_ref, *, tm):
    acc = _dot(ya_ref[...], w_ref[0 * DG:1 * DG, :].astype(BF16))
    acc += _dot(yb_ref[...], w_ref[1 * DG:2 * DG, :].astype(BF16))
    acc += _dot(yc_ref[...], w_ref[2 * DG:3 * DG, :].astype(BF16))
    acc += _dot(yd_ref[...], w_ref[3 * DG:4 * DG, :].astype(BF16))
    gate = _row_select(m_ref[...], pl.program_id(0) * tm, tm, 2, 5)
    o_ref[...] = x_ref[...] + gate * acc


def _proj_out(ya, yb, yc, yd, x, m, w_out, l):
    tm, tn = TM, 1024
    yspec = pl.BlockSpec((tm, DG), lambda i, j: (i, 0))
    return pl.pallas_call(
        functools.partial(_proj_out_kernel, tm=tm),
        grid=(NT // tm, D // tn),
        in_specs=[yspec, yspec, yspec, yspec,
                  pl.BlockSpec((None, D, tn), lambda i, j: (l, 0, j)),
                  pl.BlockSpec((tm, tn), lambda i, j: (i, j)),
                  pl.BlockSpec((None, 8, tn), lambda i, j: (3 * l + 1, 0, j))],
        out_specs=pl.BlockSpec((tm, tn), lambda i, j: (i, j)),
        out_shape=jax.ShapeDtypeStruct((NT, D), F32),
        compiler_params=_cparams(("arbitrary", "arbitrary")),
        name="proj_out",
    )(ya, yb, yc, yd, w_out, x, m)


def _final_norm_kernel(x_ref, g_ref, o_ref):
    x = x_ref[...]
    o_ref[...] = x * lax.rsqrt(jnp.mean(x * x, axis=-1, keepdims=True) + EPS) * g_ref[...]


def _final_norm(x, g):
    tm = 1024
    return pl.pallas_call(
        _final_norm_kernel,
        grid=(S // tm,),
        in_specs=[pl.BlockSpec((tm, D), lambda i: (i, 0)),
                  pl.BlockSpec((1, D), lambda i: (0, 0))],
        out_specs=pl.BlockSpec((tm, D), lambda i: (i, 0)),
        out_shape=jax.ShapeDtypeStruct((S, D), F32),
        compiler_params=_cparams(("arbitrary",)),
        name="final_norm",
    )(x, g)


def _rms_rope(t, g, cos, sin_signed, scale):
    y = t * lax.rsqrt(jnp.mean(t * t, axis=-1, keepdims=True) + EPS) * g
    lane = lax.broadcasted_iota(jnp.int32, y.shape, 1)
    partner = jnp.where((lane % 64) < 32, pltpu.roll(y, 96, 1), pltpu.roll(y, 32, 1))
    out = y * cos + partner * sin_signed
    return out * scale if scale != 1.0 else out


LOG2E = 1.4426950408889634


def _attn_prep_kernel(q_ref, k_ref, v_ref, cos_ref, sin_ref, g_ref, qo_ref, kt_ref, vo_ref):
    cos = cos_ref[...]
    sin = sin_ref[...]
    gq = g_ref[0:1, :]
    gk = g_ref[1:2, :]
    q_scale = LOG2E * ATT_HD ** -0.5
    for h in range(4):
        sl = slice(h * ATT_HD, (h + 1) * ATT_HD)
        qo_ref[:, sl] = _rms_rope(q_ref[:, sl], gq, cos, sin, q_scale).astype(BF16)
    for h in range(ATT_KVH):
        sl = slice(h * ATT_HD, (h + 1) * ATT_HD)
        kt_ref[sl, :] = _rms_rope(k_ref[:, sl], gk, cos, sin, 1.0).T.astype(BF16)
    ones = jnp.ones((v_ref.shape[0], ATT_HD), BF16)
    for h in range(ATT_KVH):
        vo_ref[:, 2 * h * ATT_HD:(2 * h + 1) * ATT_HD] = v_ref[:, h * ATT_HD:(h + 1) * ATT_HD].astype(BF16)
        vo_ref[:, (2 * h + 1) * ATT_HD:(2 * h + 2) * ATT_HD] = ones


def _attn_prep(p, cos_t, sin_t, qk_g):
    tm = 768
    return pl.pallas_call(
        _attn_prep_kernel,
        grid=(NT // tm,),
        in_specs=[pl.BlockSpec((tm, 512), lambda i: (i, 2)),
                  pl.BlockSpec((tm, 256), lambda i: (i, 6)),
                  pl.BlockSpec((tm, 256), lambda i: (i, 7)),
                  pl.BlockSpec((tm, ATT_HD), lambda i: (i, 0)),
                  pl.BlockSpec((tm, ATT_HD), lambda i: (i, 0)),
                  pl.BlockSpec((2, ATT_HD), lambda i: (0, 0))],
        out_specs=[pl.BlockSpec((tm, 512), lambda i: (i, 0)),
                   pl.BlockSpec((ATT_KVH * ATT_HD, tm), lambda i: (0, i)),
                   pl.BlockSpec((tm, 2 * ATT_KVH * ATT_HD), lambda i: (i, 0))],
        out_shape=[jax.ShapeDtypeStruct((NT, 512), BF16),
                   jax.ShapeDtypeStruct((ATT_KVH * ATT_HD, NT), BF16),
                   jax.ShapeDtypeStruct((NT, 2 * ATT_KVH * ATT_HD), BF16)],
        compiler_params=_cparams(("arbitrary",)),
        name="attn_prep",
    )(p, p, p, cos_t, sin_t, qk_g)


ATT_TQ = 256
ATT_RB = 32
ATT_CB = 384


def _stack_heads(q_ref):
    return jnp.concatenate([q_ref[:, :ATT_HD], q_ref[:, ATT_HD:]], axis=0)


def _store_heads(o_ref, o):
    o_ref[:, :ATT_HD] = o[:ATT_TQ].astype(o_ref.dtype)
    o_ref[:, ATT_HD:] = o[ATT_TQ:].astype(o_ref.dtype)


def _weighted_values(p, v1):
    ov = _dot(p, v1)
    return ov[:, :ATT_HD] / ov[:, ATT_HD:]


def _attn_ctx_kernel(q_ref, kt_ref, v_ref, o_ref):
    s = _dot(_stack_heads(q_ref), kt_ref[...])
    e = jnp.exp2(s - jnp.max(s, axis=-1, keepdims=True))
    _store_heads(o_ref, _weighted_values(e.astype(BF16), v_ref[...]))


def _attn_ctx(qn, kt, vb):
    blk = S // TCX
    return pl.pallas_call(
        _attn_ctx_kernel,
        grid=(ATT_KVH,),
        in_specs=[pl.BlockSpec((TCX, 2 * ATT_HD), lambda h: (blk, h)),
                  pl.BlockSpec((ATT_HD, TCX), lambda h: (h, blk)),
                  pl.BlockSpec((TCX, 2 * ATT_HD), lambda h: (blk, h))],
        out_specs=pl.BlockSpec((TCX, 2 * ATT_HD), lambda h: (0, h)),
        out_shape=jax.ShapeDtypeStruct((TCX, 4 * ATT_HD), BF16),
        compiler_params=_cparams(("arbitrary",)),
        name="attn_ctx",
    )(qn, kt, vb)


def _softmax_rows(s_ref, p_ref, r):
    rows = slice(r * ATT_RB, (r + 1) * ATT_RB)
    ncb = NT // ATT_CB
    mx = s_ref[rows, 0:ATT_CB]
    for c in range(1, ncb):
        mx = jnp.maximum(mx, s_ref[rows, c * ATT_CB:(c + 1) * ATT_CB])
    m = jnp.max(mx, axis=-1, keepdims=True)
    for c in range(ncb):
        cols = slice(c * ATT_CB, (c + 1) * ATT_CB)
        p_ref[rows, cols] = jnp.exp2(s_ref[rows, cols] - m).astype(BF16)


def _attn_full_kernel(q_ref, kt_ref, v_ref, o0_ref, o1_ref, s_ref, pa_ref, pb_ref):
    @pl.when(pl.program_id(0) == 0)
    def _():
        pb_ref[...] = jnp.ones_like(pb_ref)

    def stacked_q(h):
        return jnp.concatenate([q_ref[:, 2 * h * ATT_HD:(2 * h + 1) * ATT_HD],
                                q_ref[:, (2 * h + 1) * ATT_HD:(2 * h + 2) * ATT_HD]], axis=0)

    nrb = 2 * ATT_TQ // ATT_RB
    s_ref[...] = _dot(stacked_q(0), kt_ref[0:ATT_HD, :])
    _store_heads(o1_ref, _weighted_values(pb_ref[...], v_ref[:, 2 * ATT_HD:4 * ATT_HD]))
    for r in range(nrb):
        _softmax_rows(s_ref, pa_ref, r)
    s_ref[...] = _dot(stacked_q(1), kt_ref[ATT_HD:2 * ATT_HD, :])
    _store_heads(o0_ref, _weighted_values(pa_ref[...], v_ref[:, 0:2 * ATT_HD]))
    for r in range(nrb):
        _softmax_rows(s_ref, pb_ref, r)


def _attn_latent(qn, kt, vb):
    nq = S // ATT_TQ
    cur = lambda g: jnp.minimum(g, nq - 1)
    prev = lambda g: jnp.maximum(g - 1, 0)
    resident = dict(pipeline_mode=pl.Buffered(1))
    half = jax.ShapeDtypeStruct((S, 2 * ATT_HD), BF16)
    return pl.pallas_call(
        _attn_full_kernel,
        grid=(nq + 1,),
        in_specs=[pl.BlockSpec((ATT_TQ, 4 * ATT_HD), lambda g: (cur(g), 0)),
                  pl.BlockSpec((ATT_KVH * ATT_HD, NT), lambda g: (0, 0), **resident),
                  pl.BlockSpec((NT, 2 * ATT_KVH * ATT_HD), lambda g: (0, 0), **resident)],
        out_specs=[pl.BlockSpec((ATT_TQ, 2 * ATT_HD), lambda g: (cur(g), 0)),
                   pl.BlockSpec((ATT_TQ, 2 * ATT_HD), lambda g: (prev(g), 0))],
        out_shape=[half, half],
        scratch_shapes=[pltpu.VMEM((2 * ATT_TQ, NT), F32),
                        pltpu.VMEM((2 * ATT_TQ, NT), BF16),
                        pltpu.VMEM((2 * ATT_TQ, NT), BF16)],
        compiler_params=_cparams(("arbitrary",)),
        name="attn_latent",
    )(qn, kt, vb)


def _attn_mixer(p, cos_t, sin_t, qk_g):
    qn, kt, vb = _attn_prep(p, cos_t, sin_t, qk_g)
    y0, y1 = _attn_latent(qn, kt, vb)
    return jnp.concatenate([jnp.concatenate([y0, y1], axis=1), _attn_ctx(qn, kt, vb)], axis=0)


def _walk_block(s, nlat, reverse):
    if reverse:
        return jnp.where(s == 0, nlat, nlat - s)
    return jnp.where(s == 0, nlat, s - 1)


def _seq_edges(s, nlat, reverse):
    if reverse:
        return (s == 0) | (s == nlat), (s == 0) | (s == 1)
    return (s == 0) | (s == 1), (s == 0) | (s == nlat)


def _lru_kernel(xb_ref, xp_ref, xn_ref, cw_ref, cb_ref, wg_ref, bg_ref, lam_ref, *rest,
                nlat, reverse):
    if reverse:
        hf_ref, g_ref, y_ref, xe_ref, carry_ref = rest
    else:
        h_ref, xe_ref, carry_ref = rest
    s = pl.program_id(0)
    starts, ends = _seq_edges(s, nlat, reverse)

    @pl.when(s == 0)
    def _():
        carry_ref[...] = jnp.zeros_like(carry_ref)

    xe_ref[0:8, :] = jnp.where(starts, 0.0, xp_ref[...])
    xe_ref[8:8 + TB, :] = xb_ref[...]
    xe_ref[8 + TB:16 + TB, :] = jnp.where(ends, 0.0, xn_ref[...])
    u = cb_ref[...]
    for j in range(4):
        u = u + cw_ref[j:j + 1, :] * xe_ref[6 + j:6 + j + TB, :]

    gates = _dot(u.astype(BF16), wg_ref[...]) + bg_ref[...]
    r = _sigmoid(gates[:, :DG])
    i = _sigmoid(gates[:, DG:])
    a = jnp.exp((-LRU_C) * r * _softplus(-lam_ref[...]))
    b = jnp.sqrt(1.0 - a * a) * (i * u)

    row8 = lax.broadcasted_iota(jnp.int32, (TB, 1), 0) % 8
    for sh in (1, 2, 4):
        if reverse:
            a_s, b_s, valid = pltpu.roll(a, TB - sh, 0), pltpu.roll(b, TB - sh, 0), row8 < 8 - sh
        else:
            a_s, b_s, valid = pltpu.roll(a, sh, 0), pltpu.roll(b, sh, 0), row8 >= sh
        b = b + jnp.where(valid, a * b_s, 0.0)
        a = jnp.where(valid, a * a_s, a)
    carry = carry_ref[0:1, :]
    groups = [None] * (TB // 8)
    for gi in (range(TB // 8 - 1, -1, -1) if reverse else range(TB // 8)):
        hg = b[8 * gi:8 * gi + 8, :] + a[8 * gi:8 * gi + 8, :] * carry
        groups[gi] = hg
        carry = hg[0:1, :] if reverse else hg[7:8, :]
    h = jnp.concatenate(groups, axis=0)
    carry_ref[0:1, :] = carry

    if reverse:
        g = g_ref[...]
        gelu = 0.5 * g * (1.0 + jnp.tanh(0.7978845608028654 * (g + 0.044715 * (g * g * g))))
        y_ref[...] = ((hf_ref[...] + h) * gelu).astype(BF16)
    else:
        h_ref[...] = h


def _lru_pass(p, conv_w, conv_b, wg, bg, lam, nlat, reverse, hf=None):
    blk = lambda s: _walk_block(s, nlat, reverse)
    nrow8 = (nlat + 1) * TB // 8
    in_specs = [pl.BlockSpec((TB, DG), lambda s: (blk(s), 0)),
                pl.BlockSpec((8, DG), lambda s: (jnp.maximum(blk(s) * (TB // 8) - 1, 0), 0)),
                pl.BlockSpec((8, DG), lambda s: (jnp.minimum((blk(s) + 1) * (TB // 8), nrow8 - 1), 0)),
                pl.BlockSpec((4, DG), lambda s: (0, 0)),
                pl.BlockSpec((1, DG), lambda s: (0, 0)),
                pl.BlockSpec((DG, 2 * DG), lambda s: (0, 0)),
                pl.BlockSpec((1, 2 * DG), lambda s: (0, 0)),
                pl.BlockSpec((1, DG), lambda s: (0, 0))]
    args = [p, p, p, conv_w, conv_b, wg, bg, lam]
    if reverse:
        in_specs += [pl.BlockSpec((TB, DG), lambda s: (blk(s), 0)),
                     pl.BlockSpec((TB, DG), lambda s: (blk(s), 1))]
        args += [hf, p]
    return pl.pallas_call(
        functools.partial(_lru_kernel, nlat=nlat, reverse=reverse),
        grid=(nlat + 1,),
        in_specs=in_specs,
        out_specs=pl.BlockSpec((TB, DG), lambda s: (blk(s), 0)),
        out_shape=jax.ShapeDtypeStruct(((nlat + 1) * TB, DG), BF16 if reverse else F32),
        scratch_shapes=[pltpu.VMEM((TB + 16, DG), F32), pltpu.VMEM((8, DG), F32)],
        compiler_params=_cparams(("arbitrary",)),
        name="lru_bwd" if reverse else "lru_fwd",
    )(*args)


def _lru_gate_dense(w):
    eye = jnp.eye(LRU_HEADS, dtype=w.dtype)
    dense = jnp.einsum('ghij,hk->ghikj', w, eye).reshape(2, DG, DG)
    return jnp.concatenate([dense[0], dense[1]], axis=1).astype(BF16)


def _lru_mixer(p, conv_w, conv_b, gate_w, gate_b, lam, nlat=S // TB):
    cb = conv_b[None, :]
    hf = _lru_pass(p, conv_w, cb, _lru_gate_dense(gate_w[0]), gate_b[0].reshape(1, 2 * DG),
                   lam[0:1], nlat, False)
    return _lru_pass(p, conv_w, cb, _lru_gate_dense(gate_w[1]), gate_b[1].reshape(1, 2 * DG),
                     lam[1:2], nlat, True, hf=hf)


def _split3(x):
    hi = x.astype(BF16)
    r1 = x - hi.astype(F32)
    mid = r1.astype(BF16)
    lo = (r1 - mid.astype(F32)).astype(BF16)
    return hi, mid, lo


def _gla_kernel(q_ref, k_ref, v_ref, lr_ref, wl_ref, bl_ref, *rest, reverse):
    if reverse:
        of_ref, g_ref, og_ref, y_ref, st_ref, o_acc = rest
    else:
        o_acc, st_ref = rest
    s = pl.program_id(0)
    nck = TB // GLA_CHUNK
    hw = GLA_HEADS * GLA_DK

    @pl.when(s == 0)
    def _():
        st_ref[...] = jnp.zeros_like(st_ref)

    x = _dot(lr_ref[...].astype(BF16), wl_ref[...]) + bl_ref[...]
    log_a = (jnp.minimum(x, 0.0) - jnp.log(1.0 + jnp.exp(-jnp.abs(x)))) * (1.0 / GLA_TAU)

    ri = lax.broadcasted_iota(jnp.int32, (TB, TB), 0)
    ci = lax.broadcasted_iota(jnp.int32, (TB, TB), 1)
    same = (ri // GLA_CHUNK) == (ci // GLA_CHUNK)
    causal = same & ((ci >= ri) if reverse else (ci <= ri))
    tri = jnp.where(causal, 1.0, 0.0).astype(BF16)
    ones = jnp.where(same, 1.0, 0.0).astype(BF16)
    parts = _split3(log_a)
    bcum = _dot(tri, parts[0]) + _dot(tri, parts[1]) + _dot(tri, parts[2])
    tot = _dot(ones, parts[0]) + _dot(ones, parts[1]) + _dot(ones, parts[2])

    q_t = q_ref[...] * (GLA_DK ** -0.5) * jnp.exp(bcum)
    k_in = (k_ref[...] * jnp.exp(-bcum)).astype(BF16)
    k_out = (k_ref[...] * jnp.exp(tot - bcum)).astype(BF16)
    decay = jnp.exp(tot)
    lane = lax.broadcasted_iota(jnp.int32, (1, hw), 1)

    st = st_ref[...]
    for h in range(GLA_HEADS):
        head = (lane // GLA_DK) == h
        qh = jnp.where(head, q_t, 0.0).astype(BF16)
        vh = v_ref[:, h * GLA_DV:(h + 1) * GLA_DV].astype(BF16)
        att = jnp.where(causal, _dot_nt(qh, k_in), 0.0)
        o_acc[:, h * GLA_DV:(h + 1) * GLA_DV] = _dot(att.astype(BF16), vh)
    for c in (range(nck - 1, -1, -1) if reverse else range(nck)):
        rows = slice(c * GLA_CHUNK, (c + 1) * GLA_CHUNK)
        new_st = []
        for h in range(GLA_HEADS):
            head = (lane // GLA_DK) == h
            st_h = st[h * GLA_DV:(h + 1) * GLA_DV, :]
            qh = jnp.where(head, q_t[rows, :], 0.0).astype(BF16)
            cols = slice(h * GLA_DV, (h + 1) * GLA_DV)
            o_acc[rows, cols] += _dot_nt(qh, st_h.astype(BF16))
            upd = _dot_tn(v_ref[rows, cols].astype(BF16), k_out[rows, :])
            new_st.append(decay[c * GLA_CHUNK:c * GLA_CHUNK + 1, :] * st_h + jnp.where(head, upd, 0.0))
        st = jnp.concatenate(new_st, axis=0)
    st_ref[...] = st

    if reverse:
        g = g_ref[...]
        og = og_ref[...]
        for h in range(GLA_HEADS):
            cols = slice(h * GLA_DV, (h + 1) * GLA_DV)
            o = of_ref[:, cols] + o_acc[:, cols]
            o = o * lax.rsqrt(jnp.mean(o * o, axis=-1, keepdims=True) + EPS) * og
            gh = g[:, cols]
            y_ref[:, cols] = (o * (gh * _sigmoid(gh))).astype(BF16)


def _gla_pass(p, wl, bl, nlat, reverse, o_f=None, out_g=None):
    blk = lambda s: _walk_block(s, nlat, reverse)
    in_specs = [pl.BlockSpec((TB, 256), lambda s: (blk(s), 8)),
                pl.BlockSpec((TB, 256), lambda s: (blk(s), 9)),
                pl.BlockSpec((TB, DG), lambda s: (blk(s), 5)),
                pl.BlockSpec((TB, 128), lambda s: (blk(s), COL_LR // 128)),
                pl.BlockSpec((128, 256), lambda s: (0, 0)),
                pl.BlockSpec((1, 256), lambda s: (0, 0))]
    args = [p, p, p, p, wl, bl]
    scratch = [pltpu.VMEM((GLA_HEADS * GLA_DV, GLA_HEADS * GLA_DK), F32)]
    if reverse:
        in_specs += [pl.BlockSpec((TB, DG), lambda s: (blk(s), 0)),
                     pl.BlockSpec((TB, DG), lambda s: (blk(s), 6)),
                     pl.BlockSpec((1, GLA_DV), lambda s: (0, 0))]
        args += [o_f, p, out_g]
        scratch = scratch + [pltpu.VMEM((TB, DG), F32)]
    return pl.pallas_call(
        functools.partial(_gla_kernel, reverse=reverse),
        grid=(nlat + 1,),
        in_specs=in_specs,
        out_specs=pl.BlockSpec((TB, DG), lambda s: (blk(s), 0)),
        out_shape=jax.ShapeDtypeStruct(((nlat + 1) * TB, DG), BF16 if reverse else F32),
        scratch_shapes=scratch,
        compiler_params=_cparams(("arbitrary",)),
        name="gla_bwd" if reverse else "gla_fwd",
    )(*args)


def _gla_gate_dense(gate_w, d):
    w = jnp.zeros((128, GLA_HEADS * GLA_DK), F32)
    return w.at[d * GLA_RANK:(d + 1) * GLA_RANK].set(gate_w[d]).astype(BF16)


def _gla_mixer(p, gate_w, gate_b, out_g, nlat=S // TB):
    o_f = _gla_pass(p, _gla_gate_dense(gate_w, 0), gate_b[0:1], nlat, False)
    return _gla_pass(p, _gla_gate_dense(gate_w, 1), gate_b[1:2], nlat, True,
                     o_f=o_f, out_g=out_g[None, :])


FN1 = 64
FN2 = 128
FN2_TILE = 8
FK1_TILE = 8


def _dft_consts():
    def cs(n, rows=None, cols=None):
        a = 2.0 * np.pi * np.outer(np.arange(n if rows is None else rows),
                                   np.arange(n if cols is None else cols)) / n
        return np.cos(a), np.sin(a)

    c1, s1 = cs(FN1)
    m1 = np.concatenate([c1, -s1], axis=0)
    ang = 2.0 * np.pi * np.outer(np.arange(FN1), np.arange(FN2)) / (FN1 * FN2)
    tw_c = np.cos(ang).reshape(FN1, FN2 // FN2_TILE, FN2_TILE).transpose(1, 0, 2)
    tw_s = np.sin(ang).reshape(FN1, FN2 // FN2_TILE, FN2_TILE).transpose(1, 0, 2)
    c2, s2 = cs(FN2)
    m2 = np.block([[c2, s2], [-s2, c2]])
    cc, sc = cs(FNET_CH)
    eye = np.eye(DG // FNET_CH)
    ccb, scb = np.kron(eye, cc), np.kron(eye, sc)
    ch_lat = np.concatenate([ccb, scb], axis=0) / np.sqrt(FN1 * FN2 * FNET_CH)
    ch_ctx = np.concatenate([ccb, -scb], axis=1)
    ct, st = cs(TCX)
    t_ctx = np.concatenate([ct, st], axis=1) / np.sqrt(TCX * FNET_CH)
    bf = lambda a: jnp.asarray(a, F32).astype(BF16)
    return dict(m1=bf(m1), tw_c=jnp.asarray(tw_c, F32), tw_s=jnp.asarray(tw_s, F32),
                m2=bf(m2), ch_lat=bf(ch_lat), ch_ctx=bf(ch_ctx), t_ctx=bf(t_ctx))


def _fnet1_kernel(u_ref, m1_ref, tc_ref, ts_ref, z_ref):
    a = _dot(m1_ref[...], u_ref[...].astype(BF16))
    ar, ai = a[:FN1], a[FN1:]
    for n in range(FN2_TILE):
        sl = slice(n * DG, (n + 1) * DG)
        c = tc_ref[:, n:n + 1]
        sn = ts_ref[:, n:n + 1]
        z_ref[0, :, sl] = (ar[:, sl] * c + ai[:, sl] * sn).astype(BF16)
        z_ref[1, :, sl] = (ai[:, sl] * c - ar[:, sl] * sn).astype(BF16)


def _fnet2_kernel(z_ref, m2_ref, ch_ref, y_ref):
    for i in range(FK1_TILE):
        z = jnp.concatenate([z_ref[0, i], z_ref[1, i]], axis=0)
        y3 = _dot(m2_ref[...], z)
        lhs = jnp.concatenate([y3[:FN2], y3[FN2:]], axis=1).astype(BF16)
        y_ref[:, i * DG:(i + 1) * DG] = _dot(lhs, ch_ref[...]).astype(BF16)


def _fnet_ctx_kernel(u_ref, ch_ref, t_ref, y_ref):
    v =_dot(u_ref[...].astype(BF16), ch_ref[...])
    rhs = jnp.concatenate([v[:, :DG], v[:, DG:]], axis=0).astype(BF16)
    y_ref[...] = _dot(t_ref[...], rhs).astype(BF16)


def _fnet_mixer(u, consts):
    cols = FN2_TILE * DG
    z = pl.pallas_call(
        _fnet1_kernel,
        grid=(FN2 // FN2_TILE,),
        in_specs=[pl.BlockSpec((FN1, cols), lambda j: (0, j)),
                  pl.BlockSpec((2 * FN1, FN1), lambda j: (0, 0)),
                  pl.BlockSpec((None, FN1, FN2_TILE), lambda j: (j, 0, 0)),
                  pl.BlockSpec((None, FN1, FN2_TILE), lambda j: (j, 0, 0))],
        out_specs=pl.BlockSpec((2, FN1, cols), lambda j: (0, 0, j)),
        out_shape=jax.ShapeDtypeStruct((2, FN1, FN2 * DG), BF16),
        compiler_params=_cparams(("arbitrary",)),
        name="fnet_stage1",
    )(u.reshape(NT // FN2, FN2 * DG), consts['m1'], consts['tw_c'], consts['tw_s'])
    y = pl.pallas_call(
        _fnet2_kernel,
        grid=(FN1 // FK1_TILE,),
        in_specs=[pl.BlockSpec((2, FK1_TILE, FN2, DG), lambda j: (0, j, 0, 0)),
                  pl.BlockSpec((2 * FN2, 2 * FN2), lambda j: (0, 0)),
                  pl.BlockSpec((2 * DG, DG), lambda j: (0, 0))],
        out_specs=pl.BlockSpec((FN2, FK1_TILE * DG), lambda j: (0, j)),
        out_shape=jax.ShapeDtypeStruct((FN2, FN1 * DG), BF16),
        compiler_params=_cparams(("arbitrary",)),
        name="fnet_stage2",
    )(z.reshape(2, FN1, FN2, DG), consts['m2'], consts['ch_lat'])
    y_ctx = pl.pallas_call(
        _fnet_ctx_kernel,
        grid=(1,),
        in_specs=[pl.BlockSpec((TCX, DG), lambda j: (S // TCX, 0)),
                  pl.BlockSpec((DG, 2 * DG), lambda j: (0, 0)),
                  pl.BlockSpec((TCX, 2 * TCX), lambda j: (0, 0))],
        out_specs=pl.BlockSpec((TCX, DG), lambda j: (0, 0)),
        out_shape=jax.ShapeDtypeStruct((TCX, DG), BF16),
        compiler_params=_cparams(("arbitrary",)),
        name="fnet_ctx",
    )(u, consts['ch_ctx'], consts['t_ctx'])
    return jnp.concatenate([y.reshape(S, DG), y_ctx], axis=0)


def _rope_tables():
    t = np.arange(S)
    half = 32
    freqs = ROPE_THETA ** (-np.arange(half, dtype=np.float64) / half)
    ang_r = (t // GRID_W)[:, None] * freqs
    ang_c = (t % GRID_W)[:, None] * freqs
    cos = np.concatenate([np.cos(ang_r), np.cos(ang_r), np.cos(ang_c), np.cos(ang_c)], axis=1)
    sin = np.concatenate([-np.sin(ang_r), np.sin(ang_r), -np.sin(ang_c), np.sin(ang_c)], axis=1)
    cos = np.concatenate([cos, np.ones((TCX, ATT_HD))], axis=0)
    sin = np.concatenate([sin, np.zeros((TCX, ATT_HD))], axis=0)
    return jnp.asarray(cos, F32), jnp.asarray(sin, F32)


def _mod_rows(mods):
    m = mods[:, 0:2].reshape(DEPTH, 2, 3, 3, D).transpose(0, 2, 1, 3, 4).reshape(3 * DEPTH, 6, D)
    return jnp.pad(m, ((0, 0), (0, 2), (0, 0)))


def kernel(x, c, ctx, c_ctx, mod_w, mod_b, norm_g, final_g, ffn_w13, ffn_w2, w_in, w_out,
           lru_conv_w, lru_conv_b, lru_gate_w, lru_gate_b, lru_lambda, attn_qk_g,
           gla_gate_w, gla_gate_b, gla_out_g):
    xs = jnp.concatenate([x[0], ctx[0]], axis=0)
    s8 = jnp.concatenate([jax.nn.silu(c), jax.nn.silu(c_ctx)[None, :],
                          jnp.zeros((6, D), F32)], axis=0)
    m = _mod_rows(_mod_all(s8, mod_w, mod_b))
    g = norm_g.reshape(3 * DEPTH, 1, D)
    w_tail = jnp.concatenate([w_in[:, :, D_IN_MAIN:D_IN_MAIN + 2 * GLA_RANK],
                              jnp.zeros((DEPTH, D, D_P - D_IN_MAIN - 2 * GLA_RANK), F32),
                              w_in[:, :, D_IN_MAIN + 2 * GLA_RANK:]], axis=2)
    cos_t, sin_t = _rope_tables()
    consts = _dft_consts()

    for l in range(DEPTH):
        h = _ffn_a(xs, g, m, ffn_w13, l, 0)
        xs = _ffn_b(h, xs, m, ffn_w2, l, 0)
        p, u = _proj_in(xs, g, m, w_in, w_tail, l)
        ya = _lru_mixer(p, lru_conv_w[l], lru_conv_b[l], lru_gate_w[l], lru_gate_b[l], lru_lambda[l])
        yb = _attn_mixer(p, cos_t, sin_t, attn_qk_g[l])
        yc = _gla_mixer(p, gla_gate_w[l], gla_gate_b[l], gla_out_g[l])
        yd = _fnet_mixer(u, consts)
        xs = _proj_out(ya, yb, yc, yd, xs, m, w_out, l)
        h = _ffn_a(xs, g, m, ffn_w13, l, 1)
        xs = _ffn_b(h, xs, m, ffn_w2, l, 1)
    return _final_norm(xs, final_g[None, :])[None]
```

```python
import functools

import numpy as np
import jax
import jax.numpy as jnp
from jax import lax
from jax.experimental import pallas as pl
from jax.experimental.pallas import tpu as pltpu

F32 = jnp.float32
BF16 = jnp.bfloat16

D = 2048
S = 8192
TCX = 256
NT = S + TCX
DEPTH = 4
GRID_W = 64
EPS = 1e-6
D_FF = 5632
N_MOD = 9
DG = 512
LRU_HEADS = 8
LRU_HD = 64
LRU_C = 8.0
ATT_HD = 128
ATT_KVH = 2
ROPE_THETA = 10000.0
GLA_HEADS = 4
GLA_DK = 64
GLA_DV = 128
GLA_RANK = 16
GLA_TAU = 16.0
GLA_CHUNK = 64
FNET_CH = 128

D_IN_MAIN = 3584
COL_LR = 3584
D_P = 4096

V7X_VMEM_LIMIT_BYTES = 56 * 1024 * 1024
TM = 1056
TB = 256
ROW_CHUNK = 176


def _cparams(sem):
    return pltpu.CompilerParams(dimension_semantics=sem, vmem_limit_bytes=V7X_VMEM_LIMIT_BYTES)


def _dot(a, b):
    return jnp.dot(a, b, preferred_element_type=F32)


def _dot_nt(a, b):
    return lax.dot_general(a, b, (((1,), (1,)), ((), ())), preferred_element_type=F32)


def _dot_tn(a, b):
    return lax.dot_general(a, b, (((0,), (0,)), ((), ())), preferred_element_type=F32)


def _softplus(x):
    return jnp.maximum(x, 0.0) + jnp.log(1.0 + jnp.exp(-jnp.abs(x)))


def _sigmoid(x):
    return 0.5 * jnp.tanh(0.5 * x) + 0.5


def _mod_kernel(s_ref, w_ref, b_ref, o_ref):
    o_ref[...] = _dot(s_ref[...].astype(BF16), w_ref[...].astype(BF16)) + b_ref[...]


def _mod_all(s8, mod_w, mod_b):
    tn = 1024
    return pl.pallas_call(
        _mod_kernel,
        grid=(DEPTH, N_MOD * D // tn),
        in_specs=[pl.BlockSpec((8, D), lambda l, j: (0, 0)),
                  pl.BlockSpec((None, D, tn), lambda l, j: (l, 0, j)),
                  pl.BlockSpec((None, 1, tn), lambda l, j: (l, 0, j))],
        out_specs=pl.BlockSpec((None, 8, tn), lambda l, j: (l, 0, j)),
        out_shape=jax.ShapeDtypeStruct((DEPTH, 8, N_MOD * D), F32),
        compiler_params=_cparams(("arbitrary", "arbitrary")),
        name="mod_vectors",
    )(s8, mod_w, mod_b.reshape(DEPTH, 1, N_MOD * D))


def _row_select(m, row0, tm, lat_row, ctx_row):
    rows = row0 + lax.broadcasted_iota(jnp.int32, (tm, 1), 0)
    return jnp.where(rows >= S, m[ctx_row:ctx_row + 1, :], m[lat_row:lat_row + 1, :])


LANES = 128


def _norm_mod_store(x_ref, g_ref, m_ref, xn_ref, inv_ref, row0, tm):
    ones = jnp.ones((D, LANES), BF16)

    def body(c, carry):
        rows = pl.ds(pl.multiple_of(c * ROW_CHUNK, 16), ROW_CHUNK)
        x = x_ref[rows, :]
        ss = _dot((x * x).astype(BF16), ones)
        inv_ref[rows, :] = lax.rsqrt(ss * (1.0 / D) + EPS)
        return carry

    lax.fori_loop(0, tm // ROW_CHUNK, body, 0)

    is_ctx = (row0 + lax.broadcasted_iota(jnp.int32, (tm, 1), 0)) >= S
    inv = inv_ref[...]
    for c in range(D // LANES):
        cs = slice(c * LANES, (c + 1) * LANES)
        g = g_ref[:, cs]
        gain = jnp.where(is_ctx, g * (1.0 + m_ref[4:5, cs]), g * (1.0 + m_ref[1:2, cs]))
        shift = jnp.where(is_ctx, m_ref[3:4, cs], m_ref[0:1, cs])
        xn_ref[:, cs] = (x_ref[:, cs] * inv * gain + shift).astype(BF16)


def _ffn_a_kernel(x_ref, g_ref, m_ref, wg_ref, wu_ref, h_ref, xn_ref, inv_ref, *, tm):
    @pl.when(pl.program_id(1) == 0)
    def _():
        _norm_mod_store(x_ref, g_ref, m_ref, xn_ref, inv_ref, pl.program_id(0) * tm, tm)

    xn = xn_ref[...]
    gate = _dot(xn, wg_ref[...].astype(BF16))
    up = _dot(xn, wu_ref[...].astype(BF16))
    h_ref[...] = (gate * _sigmoid(gate) * up).astype(BF16)


def _ffn_a(x, g, m, ffn_w13, l, a):
    tm, tf = TM, 512
    nf = D_FF // tf
    sub = 3 * l + 2 * a
    return pl.pallas_call(
        functools.partial(_ffn_a_kernel, tm=tm),
        grid=(NT // tm, nf),
        in_specs=[pl.BlockSpec((tm, D), lambda i, j: (i, 0)),
                  pl.BlockSpec((None, 1, D), lambda i, j: (sub, 0, 0)),
                  pl.BlockSpec((None, 8, D), lambda i, j: (sub, 0, 0)),
                  pl.BlockSpec((None, None, D, tf), lambda i, j: (l, a, 0, j)),
                  pl.BlockSpec((None, None, D, tf), lambda i, j: (l, a, 0, j + nf))],
        out_specs=pl.BlockSpec((tm, tf), lambda i, j: (i, j)),
        out_shape=jax.ShapeDtypeStruct((NT, D_FF), BF16),
        scratch_shapes=[pltpu.VMEM((tm, D), BF16), pltpu.VMEM((tm, LANES), F32)],
        compiler_params=_cparams(("arbitrary", "arbitrary")),
        name="ffn_a",
    )(x, g, m, ffn_w13, ffn_w13)


def _ffn_b_kernel(h_ref, w_ref, x_ref, m_ref, o_ref, *, tm, coef):
    acc = _dot(h_ref[...], w_ref[...].astype(BF16))
    gate = _row_select(m_ref[...], pl.program_id(0) * tm, tm, 2, 5)
    o_ref[...] = x_ref[...] + (coef * gate) * acc


def _ffn_b(h, x, m, ffn_w2, l, a):
    tm, tn = TM, 256
    sub = 3 * l + 2 * a
    return pl.pallas_call(
        functools.partial(_ffn_b_kernel, tm=tm, coef=0.5),
        grid=(NT // tm, D // tn),
        in_specs=[pl.BlockSpec((tm, D_FF), lambda i, n: (i, 0)),
                  pl.BlockSpec((None, None, D_FF, tn), lambda i, n: (l, a, 0, n)),
                  pl.BlockSpec((tm, tn), lambda i, n: (i, n)),
                  pl.BlockSpec((None, 8, tn), lambda i, n: (sub, 0, n))],
        out_specs=pl.BlockSpec((tm, tn), lambda i, n: (i, n)),
        out_shape=jax.ShapeDtypeStruct((NT, D), F32),
        compiler_params=_cparams(("arbitrary", "arbitrary")),
        name="ffn_b",
    )(h, ffn_w2, x, m)


def _proj_kernel(x_ref, g_ref, m_ref, w_ref, wt_ref, p_ref, u_ref, xn_ref, inv_ref, *, tm, n_w, n_p):
    j = pl.program_id(1)

    @pl.when(j == 0)
    def _():
        _norm_mod_store(x_ref, g_ref, m_ref, xn_ref, inv_ref, pl.program_id(0) * tm, tm)

    @pl.when(j < n_w)
    def _():
        p_ref[...] = _dot(xn_ref[...], w_ref[...].astype(BF16))

    @pl.when(j == n_w)
    def _():
        p_ref[...] = _dot(xn_ref[...], wt_ref[...].astype(BF16))

    @pl.when(j == n_p)
    def _():
        u_ref[...] = _dot(xn_ref[...], wt_ref[...].astype(BF16))


def _proj_in(x, g, m, w_in, w_tail, l):
    tm, tn = TM, 512
    n_w = D_IN_MAIN // tn
    n_p = D_P // tn
    return pl.pallas_call(
        functools.partial(_proj_kernel, tm=tm, n_w=n_w, n_p=n_p),
        grid=(NT // tm, n_p + 1),
        in_specs=[pl.BlockSpec((tm, D), lambda i, j: (i, 0)),
                  pl.BlockSpec((None, 1, D), lambda i, j: (3 * l + 1, 0, 0)),
                  pl.BlockSpec((None, 8, D), lambda i, j: (3 * l + 1, 0, 0)),
                  pl.BlockSpec((None, D, tn), lambda i, j: (l, 0, jnp.minimum(j, n_w - 1))),
                  pl.BlockSpec((None, D, tn), lambda i, j: (l, 0, jnp.maximum(j - n_w, 0)))],
        out_specs=[pl.BlockSpec((tm, tn), lambda i, j: (i, jnp.minimum(j, n_p - 1))),
                   pl.BlockSpec((tm, DG), lambda i, j: (i, 0))],
        out_shape=[jax.ShapeDtypeStruct((NT, D_P), F32),
                   jax.ShapeDtypeStruct((NT, DG), F32)],
        scratch_shapes=[pltpu.VMEM((tm, D), BF16), pltpu.VMEM((tm, LANES), F32)],
        compiler_params=_cparams(("arbitrary", "arbitrary")),
        name="proj_in",
    )(x, g, m, w_in, w_tail)


def _proj_out_kernel(ya_ref, yb_ref, yc_ref, yd_ref, w_ref, x_ref, m_ref, o_ref, *, tm):
    acc = _dot(ya_ref[...], w_ref[0 * DG:1 * DG, :].astype(BF16))
    acc += _dot(yb_ref[...], w_ref[1 * DG:2 * DG, :].astype(BF16))
    acc += _dot(yc_ref[...], w_ref[2 * DG:3 * DG, :].astype(BF16))
    acc += _dot(yd_ref[...], w_ref[3 * DG:4 * DG, :].astype(BF16))
    gate = _row_select(m_ref[...], pl.program_id(0) * tm, tm, 2, 5)
    o_ref[...] = x_ref[...] + gate * acc


def _proj_out(ya, yb, yc, yd, x, m, w_out, l):
    tm, tn = TM, 1024
    yspec = pl.BlockSpec((tm, DG), lambda i, j: (i, 0))
    return pl.pallas_call(
        functools.partial(_proj_out_kernel, tm=tm),
        grid=(NT // tm, D // tn),
        in_specs=[yspec, yspec, yspec, yspec,
                  pl.BlockSpec((None, D, tn), lambda i, j: (l, 0, j)),
                  pl.BlockSpec((tm, tn), lambda i, j: (i, j)),
                  pl.BlockSpec((None, 8, tn), lambda i, j: (3 * l + 1, 0, j))],
        out_specs=pl.BlockSpec((tm, tn), lambda i, j: (i, j)),
        out_shape=jax.ShapeDtypeStruct((NT, D), F32),
        compiler_params=_cparams(("arbitrary", "arbitrary")),
        name="proj_out",
    )(ya, yb, yc, yd, w_out, x, m)


def _final_norm_kernel(x_ref, g_ref, o_ref):
    x = x_ref[...]
    o_ref[...] = x * lax.rsqrt(jnp.mean(x * x, axis=-1, keepdims=True) + EPS) * g_ref[...]


def _final_norm(x, g):
    tm = 1024
    return pl.pallas_call(
        _final_norm_kernel,
        grid=(S // tm,),
        in_specs=[pl.BlockSpec((tm, D), lambda i: (i, 0)),
                  pl.BlockSpec((1, D), lambda i: (0, 0))],
        out_specs=pl.BlockSpec((tm, D), lambda i: (i, 0)),
        out_shape=jax.ShapeDtypeStruct((S, D), F32),
        compiler_params=_cparams(("arbitrary",)),
        name="final_norm",
    )(x, g)


def _rms_rope(t, g, cos, sin_signed, scale):
    y = t * lax.rsqrt(jnp.mean(t * t, axis=-1, keepdims=True) + EPS) * g
    lane = lax.broadcasted_iota(jnp.int32, y.shape, 1)
    partner = jnp.where((lane % 64) < 32, pltpu.roll(y, 96, 1), pltpu.roll(y, 32, 1))
    out = y * cos + partner * sin_signed
    return out * scale if scale != 1.0 else out


LOG2E = 1.4426950408889634


def _attn_prep_kernel(q_ref, k_ref, v_ref, cos_ref, sin_ref, g_ref, qo_ref, kt_ref, vo_ref):
    cos = cos_ref[...]
    sin = sin_ref[...]
    gq = g_ref[0:1, :]
    gk = g_ref[1:2, :]
    q_scale = LOG2E * ATT_HD ** -0.5
    for h in range(4):
        sl = slice(h * ATT_HD, (h + 1) * ATT_HD)
        qo_ref[:, sl] = _rms_rope(q_ref[:, sl], gq, cos, sin, q_scale).astype(BF16)
    for h in range(ATT_KVH):
        sl = slice(h * ATT_HD, (h + 1) * ATT_HD)
        kt_ref[sl, :] = _rms_rope(k_ref[:, sl], gk, cos, sin, 1.0).T.astype(BF16)
    ones = jnp.ones((v_ref.shape[0], ATT_HD), BF16)
    for h in range(ATT_KVH):
        vo_ref[:, 2 * h * ATT_HD:(2 * h + 1) * ATT_HD] = v_ref[:, h * ATT_HD:(h + 1) * ATT_HD].astype(BF16)
        vo_ref[:, (2 * h + 1) * ATT_HD:(2 * h + 2) * ATT_HD] = ones


def _attn_prep(p, cos_t, sin_t, qk_g):
    tm = 768
    return pl.pallas_call(
        _attn_prep_kernel,
        grid=(NT // tm,),
        in_specs=[pl.BlockSpec((tm, 512), lambda i: (i, 2)),
                  pl.BlockSpec((tm, 256), lambda i: (i, 6)),
                  pl.BlockSpec((tm, 256), lambda i: (i, 7)),
                  pl.BlockSpec((tm, ATT_HD), lambda i: (i, 0)),
                  pl.BlockSpec((tm, ATT_HD), lambda i: (i, 0)),
                  pl.BlockSpec((2, ATT_HD), lambda i: (0, 0))],
        out_specs=[pl.BlockSpec((tm, 512), lambda i: (i, 0)),
                   pl.BlockSpec((ATT_KVH * ATT_HD, tm), lambda i: (0, i)),
                   pl.BlockSpec((tm, 2 * ATT_KVH * ATT_HD), lambda i: (i, 0))],
        out_shape=[jax.ShapeDtypeStruct((NT, 512), BF16),
                   jax.ShapeDtypeStruct((ATT_KVH * ATT_HD, NT), BF16),
                   jax.ShapeDtypeStruct((NT, 2 * ATT_KVH * ATT_HD), BF16)],
        compiler_params=_cparams(("arbitrary",)),
        name="attn_prep",
    )(p, p, p, cos_t, sin_t, qk_g)


ATT_TQ = 128
ATT_RB = 32
ATT_CB = 384


def _stack_heads(q_ref):
    return jnp.concatenate([q_ref[:, :ATT_HD], q_ref[:, ATT_HD:]], axis=0)


def _store_heads(o_ref, o):
    half = o.shape[0] // 2
    o_ref[:, :ATT_HD] = o[:half].astype(o_ref.dtype)
    o_ref[:, ATT_HD:] = o[half:].astype(o_ref.dtype)


def _weighted_values(p, v1):
    ov = _dot(p, v1)
    return ov[:, :ATT_HD] / ov[:, ATT_HD:]


def _attn_ctx_kernel(q_ref, kt_ref, v_ref, o_ref):
    s = _dot(_stack_heads(q_ref), kt_ref[...])
    e = jnp.exp2(s - jnp.max(s, axis=-1, keepdims=True))
    _store_heads(o_ref, _weighted_values(e.astype(BF16), v_ref[...]))


def _attn_ctx(qn, kt, vb):
    blk = S // TCX
    return pl.pallas_call(
        _attn_ctx_kernel,
        grid=(ATT_KVH,),
        in_specs=[pl.BlockSpec((TCX, 2 * ATT_HD), lambda h: (blk, h)),
                  pl.BlockSpec((ATT_HD, TCX), lambda h: (h, blk)),
                  pl.BlockSpec((TCX, 2 * ATT_HD), lambda h: (blk, h))],
        out_specs=pl.BlockSpec((TCX, 2 * ATT_HD), lambda h: (0, h)),
        out_shape=jax.ShapeDtypeStruct((TCX, 4 * ATT_HD), BF16),
        compiler_params=_cparams(("arbitrary",)),
        name="attn_ctx",
    )(qn, kt, vb)


def _softmax_rows(s_ref, p_ref, r):
    rows = slice(r * ATT_RB, (r + 1) * ATT_RB)
    ncb = NT // ATT_CB
    mx = s_ref[rows, 0:ATT_CB]
    for c in range(1, ncb):
        mx = jnp.maximum(mx, s_ref[rows, c * ATT_CB:(c + 1) * ATT_CB])
    m = jnp.max(mx, axis=-1, keepdims=True)
    for c in range(ncb):
        cols = slice(c * ATT_CB, (c + 1) * ATT_CB)
        p_ref[rows, cols] = jnp.exp2(s_ref[rows, cols] - m).astype(BF16)


def _attn_full_kernel(q_ref, kt_ref, v_ref, o0_ref, o1_ref, sa_ref, sb_ref, pa_ref, pb_ref):
    @pl.when(pl.program_id(0) == 0)
    def _():
        sb_ref[...] = jnp.zeros_like(sb_ref)

    def stacked_q(h):
        return jnp.concatenate([q_ref[:, 2 * h * ATT_HD:(2 * h + 1) * ATT_HD],
                                q_ref[:, (2 * h + 1) * ATT_HD:(2 * h + 2) * ATT_HD]], axis=0)

    nrb = 2 * ATT_TQ // ATT_RB
    sa_ref[...] = _dot(stacked_q(0), kt_ref[0:ATT_HD, :])
    for r in range(nrb):
        _softmax_rows(sb_ref, pb_ref, r)
    _store_heads(o1_ref, _weighted_values(pb_ref[...], v_ref[:, 2 * ATT_HD:4 * ATT_HD]))
    for r in range(nrb):
        _softmax_rows(sa_ref, pa_ref, r)
    sb_ref[...] = _dot(stacked_q(1), kt_ref[ATT_HD:2 * ATT_HD, :])
    _store_heads(o0_ref, _weighted_values(pa_ref[...], v_ref[:, 0:2 * ATT_HD]))


def _attn_latent(qn, kt, vb):
    nq = S // ATT_TQ
    cur = lambda g: jnp.minimum(g, nq - 1)
    prev = lambda g: jnp.maximum(g - 1, 0)
    resident = dict(pipeline_mode=pl.Buffered(1))
    half = jax.ShapeDtypeStruct((S, 2 * ATT_HD), BF16)
    return pl.pallas_call(
        _attn_full_kernel,
        grid=(nq + 1,),
        in_specs=[pl.BlockSpec((ATT_TQ, 4 * ATT_HD), lambda g: (cur(g), 0)),
                  pl.BlockSpec((ATT_KVH * ATT_HD, NT), lambda g: (0, 0), **resident),
                  pl.BlockSpec((NT, 2 * ATT_KVH * ATT_HD), lambda g: (0, 0), **resident)],
        out_specs=[pl.BlockSpec((ATT_TQ, 2 * ATT_HD), lambda g: (cur(g), 0)),
                   pl.BlockSpec((ATT_TQ, 2 * ATT_HD), lambda g: (prev(g), 0))],
        out_shape=[half, half],
        scratch_shapes=[pltpu.VMEM((2 * ATT_TQ, NT), F32),
                        pltpu.VMEM((2 * ATT_TQ, NT), F32),
                        pltpu.VMEM((2 * ATT_TQ, NT), BF16),
                        pltpu.VMEM((2 * ATT_TQ, NT), BF16)],
        compiler_params=_cparams(("arbitrary",)),
        name="attn_latent",
    )(qn, kt, vb)


def _attn_mixer(p, cos_t, sin_t, qk_g):
    qn, kt, vb = _attn_prep(p, cos_t, sin_t, qk_g)
    y0, y1 = _attn_latent(qn, kt, vb)
    return jnp.concatenate([jnp.concatenate([y0, y1], axis=1), _attn_ctx(qn, kt, vb)], axis=0)


def _walk_block(s, nlat, reverse):
    if reverse:
        return jnp.where(s == 0, nlat, nlat - s)
    return jnp.where(s == 0, nlat, s - 1)


def _seq_edges(s, nlat, reverse):
    if reverse:
        return (s == 0) | (s == nlat), (s == 0) | (s == 1)
    return (s == 0) | (s == 1), (s == 0) | (s == nlat)


def _lru_kernel(xb_ref, xp_ref, xn_ref, cw_ref, cb_ref, wg_ref, bg_ref, lam_ref, *rest,
                nlat, reverse):
    if reverse:
        hf_ref, g_ref, y_ref, xe_ref, carry_ref = rest
    else:
        h_ref, xe_ref, carry_ref = rest
    s = pl.program_id(0)
    starts, ends = _seq_edges(s, nlat, reverse)

    @pl.when(s == 0)
    def _():
        carry_ref[...] = jnp.zeros_like(carry_ref)

    xe_ref[0:8, :] = jnp.where(starts, 0.0, xp_ref[...])
    xe_ref[8:8 + TB, :] = xb_ref[...]
    xe_ref[8 + TB:16 + TB, :] = jnp.where(ends, 0.0, xn_ref[...])
    u = cb_ref[...]
    for j in range(4):
        u = u + cw_ref[j:j + 1, :] * xe_ref[6 + j:6 + j + TB, :]

    gates = _dot(u.astype(BF16), wg_ref[...]) + bg_ref[...]
    r = _sigmoid(gates[:, :DG])
    i = _sigmoid(gates[:, DG:])
    a = jnp.exp((-LRU_C) * r * _softplus(-lam_ref[...]))
    b = jnp.sqrt(1.0 - a * a) * (i * u)

    row8 = lax.broadcasted_iota(jnp.int32, (TB, 1), 0) % 8
    for sh in (1, 2, 4):
        if reverse:
            a_s, b_s, valid = pltpu.roll(a, TB - sh, 0), pltpu.roll(b, TB - sh, 0), row8 < 8 - sh
        else:
            a_s, b_s, valid = pltpu.roll(a, sh, 0), pltpu.roll(b, sh, 0), row8 >= sh
        b = b + jnp.where(valid, a * b_s, 0.0)
        a = jnp.where(valid, a * a_s, a)
    carry = carry_ref[0:1, :]
    groups = [None] * (TB // 8)
    for gi in (range(TB // 8 - 1, -1, -1) if reverse else range(TB // 8)):
        hg = b[8 * gi:8 * gi + 8, :] + a[8 * gi:8 * gi + 8, :] * carry
        groups[gi] = hg
        carry = hg[0:1, :] if reverse else hg[7:8, :]
    h = jnp.concatenate(groups, axis=0)
    carry_ref[0:1, :] = carry

    if reverse:
        g = g_ref[...]
        gelu = 0.5 * g * (1.0 + jnp.tanh(0.7978845608028654 * (g + 0.044715 * (g * g * g))))
        y_ref[...] = ((hf_ref[...] + h) * gelu).astype(BF16)
    else:
        h_ref[...] = h


def _lru_pass(p, conv_w, conv_b, wg, bg, lam, nlat, reverse, hf=None):
    blk = lambda s: _walk_block(s, nlat, reverse)
    nrow8 = (nlat + 1) * TB // 8
    in_specs = [pl.BlockSpec((TB, DG), lambda s: (blk(s), 0)),
                pl.BlockSpec((8, DG), lambda s: (jnp.maximum(blk(s) * (TB // 8) - 1, 0), 0)),
                pl.BlockSpec((8, DG), lambda s: (jnp.minimum((blk(s) + 1) * (TB // 8), nrow8 - 1), 0)),
                pl.BlockSpec((4, DG), lambda s: (0, 0)),
                pl.BlockSpec((1, DG), lambda s: (0, 0)),
                pl.BlockSpec((DG, 2 * DG), lambda s: (0, 0)),
                pl.BlockSpec((1, 2 * DG), lambda s: (0, 0)),
                pl.BlockSpec((1, DG), lambda s: (0, 0))]
    args = [p, p, p, conv_w, conv_b, wg, bg, lam]
    if reverse:
        in_specs += [pl.BlockSpec((TB, DG), lambda s: (blk(s), 0)),
                     pl.BlockSpec((TB, DG), lambda s: (blk(s), 1))]
        args += [hf, p]
    return pl.pallas_call(
        functools.partial(_lru_kernel, nlat=nlat, reverse=reverse),
        grid=(nlat + 1,),
        in_specs=in_specs,
        out_specs=pl.BlockSpec((TB, DG), lambda s: (blk(s), 0)),
        out_shape=jax.ShapeDtypeStruct(((nlat + 1) * TB, DG), BF16 if reverse else F32),
        scratch_shapes=[pltpu.VMEM((TB + 16, DG), F32), pltpu.VMEM((8, DG), F32)],
        compiler_params=_cparams(("arbitrary",)),
        name="lru_bwd" if reverse else "lru_fwd",
    )(*args)


def _lru_gate_dense(w):
    eye = jnp.eye(LRU_HEADS, dtype=w.dtype)
    dense = jnp.einsum('ghij,hk->ghikj', w, eye).reshape(2, DG, DG)
    return jnp.concatenate([dense[0], dense[1]], axis=1).astype(BF16)


def _lru_mixer(p, conv_w, conv_b, gate_w, gate_b, lam, nlat=S // TB):
    cb = conv_b[None, :]
    hf = _lru_pass(p, conv_w, cb, _lru_gate_dense(gate_w[0]), gate_b[0].reshape(1, 2 * DG),
                   lam[0:1], nlat, False)
    return _lru_pass(p, conv_w, cb, _lru_gate_dense(gate_w[1]), gate_b[1].reshape(1, 2 * DG),
                     lam[1:2], nlat, True, hf=hf)


def _split3(x):
    hi = x.astype(BF16)
    r1 = x - hi.astype(F32)
    mid = r1.astype(BF16)
    lo = (r1 - mid.astype(F32)).astype(BF16)
    return hi, mid, lo


def _gla_kernel(q_ref, k_ref, v_ref, lr_ref, wl_ref, bl_ref, *rest, reverse):
    if reverse:
        of_ref, g_ref, og_ref, y_ref, st_ref, o_acc = rest
    else:
        o_acc, st_ref = rest
    s = pl.program_id(0)
    nck = TB // GLA_CHUNK
    hw = GLA_HEADS * GLA_DK

    @pl.when(s == 0)
    def _():
        st_ref[...] = jnp.zeros_like(st_ref)

    x = _dot(lr_ref[...].astype(BF16), wl_ref[...]) + bl_ref[...]
    log_a = (jnp.minimum(x, 0.0) - jnp.log(1.0 + jnp.exp(-jnp.abs(x)))) * (1.0 / GLA_TAU)

    ri = lax.broadcasted_iota(jnp.int32, (TB, TB), 0)
    ci = lax.broadcasted_iota(jnp.int32, (TB, TB), 1)
    same = (ri // GLA_CHUNK) == (ci // GLA_CHUNK)
    causal = same & ((ci >= ri) if reverse else (ci <= ri))
    tri = jnp.where(causal, 1.0, 0.0).astype(BF16)
    ones = jnp.where(same, 1.0, 0.0).astype(BF16)
    parts = _split3(log_a)
    bcum = _dot(tri, parts[0]) + _dot(tri, parts[1]) + _dot(tri, parts[2])
    tot = _dot(ones, parts[0]) + _dot(ones, parts[1]) + _dot(ones, parts[2])

    q_t = q_ref[...] * (GLA_DK ** -0.5) * jnp.exp(bcum)
    k_in = (k_ref[...] * jnp.exp(-bcum)).astype(BF16)
    k_out = (k_ref[...] * jnp.exp(tot - bcum)).astype(BF16)
    decay = jnp.exp(tot)
    lane = lax.broadcasted_iota(jnp.int32, (1, hw), 1)

    st = st_ref[...]
    for h in range(GLA_HEADS):
        head = (lane // GLA_DK) == h
        qh = jnp.where(head, q_t, 0.0).astype(BF16)
        vh = v_ref[:, h * GLA_DV:(h + 1) * GLA_DV].astype(BF16)
        att = jnp.where(causal, _dot_nt(qh, k_in), 0.0)
        o_acc[:, h * GLA_DV:(h + 1) * GLA_DV] = _dot(att.astype(BF16), vh)
    for c in (range(nck - 1, -1, -1) if reverse else range(nck)):
        rows = slice(c * GLA_CHUNK, (c + 1) * GLA_CHUNK)
        new_st = []
        for h in range(GLA_HEADS):
            head = (lane // GLA_DK) == h
            st_h = st[h * GLA_DV:(h + 1) * GLA_DV, :]
            qh = jnp.where(head, q_t[rows, :], 0.0).astype(BF16)
            cols = slice(h * GLA_DV, (h + 1) * GLA_DV)
            o_acc[rows, cols] += _dot_nt(qh, st_h.astype(BF16))
            upd = _dot_tn(v_ref[rows, cols].astype(BF16), k_out[rows, :])
            new_st.append(decay[c * GLA_CHUNK:c * GLA_CHUNK + 1, :] * st_h + jnp.where(head, upd, 0.0))
        st = jnp.concatenate(new_st, axis=0)
    st_ref[...] = st

    if reverse:
        g = g_ref[...]
        og = og_ref[...]
        for h in range(GLA_HEADS):
            cols = slice(h * GLA_DV, (h + 1) * GLA_DV)
            o = of_ref[:, cols] + o_acc[:, cols]
            o = o * lax.rsqrt(jnp.mean(o * o, axis=-1, keepdims=True) + EPS) * og
            gh = g[:, cols]
            y_ref[:, cols] = (o * (gh * _sigmoid(gh))).astype(BF16)


def _gla_pass(p, wl, bl, nlat, reverse, o_f=None, out_g=None):
    blk = lambda s: _walk_block(s, nlat, reverse)
    in_specs = [pl.BlockSpec((TB, 256), lambda s: (blk(s), 8)),
                pl.BlockSpec((TB, 256), lambda s: (blk(s), 9)),
                pl.BlockSpec((TB, DG), lambda s: (blk(s), 5)),
                pl.BlockSpec((TB, 128), lambda s: (blk(s), COL_LR // 128)),
                pl.BlockSpec((128, 256), lambda s: (0, 0)),
                pl.BlockSpec((1, 256), lambda s: (0, 0))]
    args = [p, p, p, p, wl, bl]
    scratch = [pltpu.VMEM((GLA_HEADS * GLA_DV, GLA_HEADS * GLA_DK), F32)]
    if reverse:
        in_specs += [pl.BlockSpec((TB, DG), lambda s: (blk(s), 0)),
                     pl.BlockSpec((TB, DG), lambda s: (blk(s), 6)),
                     pl.BlockSpec((1, GLA_DV), lambda s: (0, 0))]
        args += [o_f, p, out_g]
        scratch = scratch + [pltpu.VMEM((TB, DG), F32)]
    return pl.pallas_call(
        functools.partial(_gla_kernel, reverse=reverse),
        grid=(nlat + 1,),
        in_specs=in_specs,
        out_specs=pl.BlockSpec((TB, DG), lambda s: (blk(s), 0)),
        out_shape=jax.ShapeDtypeStruct(((nlat + 1) * TB, DG), BF16 if reverse else F32),
        scratch_shapes=scratch,
        compiler_params=_cparams(("arbitrary",)),
        name="gla_bwd" if reverse else "gla_fwd",
    )(*args)


def _gla_gate_dense(gate_w, d):
    w = jnp.zeros((128, GLA_HEADS * GLA_DK), F32)
    return w.at[d * GLA_RANK:(d + 1) * GLA_RANK].set(gate_w[d]).astype(BF16)


def _gla_mixer(p, gate_w, gate_b, out_g, nlat=S // TB):
    o_f = _gla_pass(p, _gla_gate_dense(gate_w, 0), gate_b[0:1], nlat, False)
    return _gla_pass(p, _gla_gate_dense(gate_w, 1), gate_b[1:2], nlat, True,
                     o_f=o_f, out_g=out_g[None, :])


FN1 = 64
FN2 = 128
FN2_TILE = 8
FK1_TILE = 8


def _dft_consts():
    def cs(n, rows=None, cols=None):
        a = 2.0 * np.pi * np.outer(np.arange(n if rows is None else rows),
                                   np.arange(n if cols is None else cols)) / n
        return np.cos(a), np.sin(a)

    c1, s1 = cs(FN1)
    m1 = np.concatenate([c1, -s1], axis=0)
    ang = 2.0 * np.pi * np.outer(np.arange(FN1), np.arange(FN2)) / (FN1 * FN2)
    tw_c = np.cos(ang).reshape(FN1, FN2 // FN2_TILE, FN2_TILE).transpose(1, 0, 2)
    tw_s = np.sin(ang).reshape(FN1, FN2 // FN2_TILE, FN2_TILE).transpose(1, 0, 2)
    c2, s2 = cs(FN2)
    m2 = np.block([[c2, s2], [-s2, c2]])
    cc, sc = cs(FNET_CH)
    eye = np.eye(DG // FNET_CH)
    ccb, scb = np.kron(eye, cc), np.kron(eye, sc)
    ch_lat = np.concatenate([ccb, scb], axis=0) / np.sqrt(FN1 * FN2 * FNET_CH)
    ch_ctx = np.concatenate([ccb, -scb], axis=1)
    ct, st = cs(TCX)
    t_ctx = np.concatenate([ct, st], axis=1) / np.sqrt(TCX * FNET_CH)
    bf = lambda a: jnp.asarray(a, F32).astype(BF16)
    return dict(m1=bf(m1), tw_c=jnp.asarray(tw_c, F32), tw_s=jnp.asarray(tw_s, F32),
                m2=bf(m2), ch_lat=bf(ch_lat), ch_ctx=bf(ch_ctx), t_ctx=bf(t_ctx))


def _fnet1_kernel(u_ref, m1_ref, tc_ref, ts_ref, z_ref):
    a = _dot(m1_ref[...], u_ref[...].astype(BF16))
    ar, ai = a[:FN1], a[FN1:]
    for n in range(FN2_TILE):
        sl = slice(n * DG, (n + 1) * DG)
        c = tc_ref[:, n:n + 1]
        sn = ts_ref[:, n:n + 1]
        z_ref[0, :, sl] = (ar[:, sl] * c + ai[:, sl] * sn).astype(BF16)
        z_ref[1, :, sl] = (ai[:, sl] * c - ar[:, sl] * sn).astype(BF16)


def _fnet2_kernel(z_ref, m2_ref, ch_ref, y_ref):
    for i in range(FK1_TILE):
        z = jnp.concatenate([z_ref[0, i], z_ref[1, i]], axis=0)
        y3 = _dot(m2_ref[...], z)
        lhs = jnp.concatenate([y3[:FN2], y3[FN2:]], axis=1).astype(BF16)
        y_ref[:, i * DG:(i + 1) * DG] = _dot(lhs, ch_ref[...]).astype(BF16)


def _fnet_ctx_kernel(u_ref, ch_ref, t_ref, y_ref):
    v =_dot(u_ref[...].astype(BF16), ch_ref[...])
    rhs = jnp.concatenate([v[:, :DG], v[:, DG:]], axis=0).astype(BF16)
    y_ref[...] = _dot(t_ref[...], rhs).astype(BF16)


def _fnet_mixer(u, consts):
    cols = FN2_TILE * DG
    z = pl.pallas_call(
        _fnet1_kernel,
        grid=(FN2 // FN2_TILE,),
        in_specs=[pl.BlockSpec((FN1, cols), lambda j: (0, j)),
                  pl.BlockSpec((2 * FN1, FN1), lambda j: (0, 0)),
                  pl.BlockSpec((None, FN1, FN2_TILE), lambda j: (j, 0, 0)),
                  pl.BlockSpec((None, FN1, FN2_TILE), lambda j: (j, 0, 0))],
        out_specs=pl.BlockSpec((2, FN1, cols), lambda j: (0, 0, j)),
        out_shape=jax.ShapeDtypeStruct((2, FN1, FN2 * DG), BF16),
        compiler_params=_cparams(("arbitrary",)),
        name="fnet_stage1",
    )(u.reshape(NT // FN2, FN2 * DG), consts['m1'], consts['tw_c'], consts['tw_s'])
    y = pl.pallas_call(
        _fnet2_kernel,
        grid=(FN1 // FK1_TILE,),
        in_specs=[pl.BlockSpec((2, FK1_TILE, FN2, DG), lambda j: (0, j, 0, 0)),
                  pl.BlockSpec((2 * FN2, 2 * FN2), lambda j: (0, 0)),
                  pl.BlockSpec((2 * DG, DG), lambda j: (0, 0))],
        out_specs=pl.BlockSpec((FN2, FK1_TILE * DG), lambda j: (0, j)),
        out_shape=jax.ShapeDtypeStruct((FN2, FN1 * DG), BF16),
        compiler_params=_cparams(("arbitrary",)),
        name="fnet_stage2",
    )(z.reshape(2, FN1, FN2, DG), consts['m2'], consts['ch_lat'])
    y_ctx = pl.pallas_call(
        _fnet_ctx_kernel,
        grid=(1,),
        in_specs=[pl.BlockSpec((TCX, DG), lambda j: (S // TCX, 0)),
                  pl.BlockSpec((DG, 2 * DG), lambda j: (0, 0)),
                  pl.BlockSpec((TCX, 2 * TCX), lambda j: (0, 0))],
        out_specs=pl.BlockSpec((TCX, DG), lambda j: (0, 0)),
        out_shape=jax.ShapeDtypeStruct((TCX, DG), BF16),
        compiler_params=_cparams(("arbitrary",)),
        name="fnet_ctx",
    )(u, consts['ch_ctx'], consts['t_ctx'])
    return jnp.concatenate([y.reshape(S, DG), y_ctx], axis=0)


def _rope_tables():
    t = np.arange(S)
    half = 32
    freqs = ROPE_THETA ** (-np.arange(half, dtype=np.float64) / half)
    ang_r = (t // GRID_W)[:, None] * freqs
    ang_c = (t % GRID_W)[:, None] * freqs
    cos = np.concatenate([np.cos(ang_r), np.cos(ang_r), np.cos(ang_c), np.cos(ang_c)], axis=1)
    sin = np.concatenate([-np.sin(ang_r), np.sin(ang_r), -np.sin(ang_c), np.sin(ang_c)], axis=1)
    cos = np.concatenate([cos, np.ones((TCX, ATT_HD))], axis=0)
    sin = np.concatenate([sin, np.zeros((TCX, ATT_HD))], axis=0)
    return jnp.asarray(cos, F32), jnp.asarray(sin, F32)


def _mod_rows(mods):
    m = mods[:, 0:2].reshape(DEPTH, 2, 3, 3, D).transpose(0, 2, 1, 3, 4).reshape(3 * DEPTH, 6, D)
    return jnp.pad(m, ((0, 0), (0, 2), (0, 0)))


def kernel(x, c, ctx, c_ctx, mod_w, mod_b, norm_g, final_g, ffn_w13, ffn_w2, w_in, w_out,
           lru_conv_w, lru_conv_b, lru_gate_w, lru_gate_b, lru_lambda, attn_qk_g,
           gla_gate_w, gla_gate_b, gla_out_g):
    xs = jnp.concatenate([x[0], ctx[0]], axis=0)
    s8 = jnp.concatenate([jax.nn.silu(c), jax.nn.silu(c_ctx)[None, :],
                          jnp.zeros((6, D), F32)], axis=0)
    m = _mod_rows(_mod_all(s8, mod_w, mod_b))
    g = norm_g.reshape(3 * DEPTH, 1, D)
    w_tail = jnp.concatenate([w_in[:, :, D_IN_MAIN:D_IN_MAIN + 2 * GLA_RANK],
                              jnp.zeros((DEPTH, D, D_P - D_IN_MAIN - 2 * GLA_RANK), F32),
                              w_in[:, :, D_IN_MAIN + 2 * GLA_RANK:]], axis=2)
    cos_t, sin_t = _rope_tables()
    consts = _dft_consts()

    for l in range(DEPTH):
        h = _ffn_a(xs, g, m, ffn_w13, l, 0)
        xs = _ffn_b(h, xs, m, ffn_w2, l, 0)
        p, u = _proj_in(xs, g, m, w_in, w_tail, l)
        ya = _lru_mixer(p, lru_conv_w[l], lru_conv_b[l], lru_gate_w[l], lru_gate_b[l], lru_lambda[l])
        yb = _attn_mixer(p, cos_t, sin_t, attn_qk_g[l])
        yc = _gla_mixer(p, gla_gate_w[l], gla_gate_b[l], gla_out_g[l])
        yd = _fnet_mixer(u, consts)
        xs = _proj_out(ya, yb, yc, yd, xs, m, w_out, l)
        h = _ffn_a(xs, g, m, ffn_w13, l, 1)
        xs = _ffn_b(h, xs, m, ffn_w2, l, 1)
    return _final_norm(xs, final_g[None, :])[None]
```

```python
import functools

import numpy as np
import jax
import jax.numpy as jnp
from jax import lax
from jax.experimental import pallas as pl
from jax.experimental.pallas import tpu as pltpu

F32 = jnp.float32
BF16 = jnp.bfloat16

D = 2048
S = 8192
TCX = 256
NT = S + TCX
DEPTH = 4
GRID_W = 64
EPS = 1e-6
D_FF = 5632
N_MOD = 9
DG = 512
LRU_HEADS = 8
LRU_HD = 64
LRU_C = 8.0
ATT_HD = 128
ATT_KVH = 2
ROPE_THETA = 10000.0
GLA_HEADS = 4
GLA_DK = 64
GLA_DV = 128
GLA_RANK = 16
GLA_TAU = 16.0
GLA_CHUNK = 64
FNET_CH = 128

D_IN_MAIN = 3584
W_TAIL = 640

V7X_VMEM_LIMIT_BYTES = 56 * 1024 * 1024
TM = 1056
TB = 256
ROW_CHUNK = 176


def _cparams(sem):
    return pltpu.CompilerParams(dimension_semantics=sem, vmem_limit_bytes=V7X_VMEM_LIMIT_BYTES)


def _dot(a, b):
    return jnp.dot(a, b, preferred_element_type=F32)


def _dot_nt(a, b):
    return lax.dot_general(a, b, (((1,), (1,)), ((), ())), preferred_element_type=F32)


def _dot_tn(a, b):
    return lax.dot_general(a, b, (((0,), (0,)), ((), ())), preferred_element_type=F32)


def _softplus(x):
    return jnp.maximum(x, 0.0) + jnp.log(1.0 + jnp.exp(-jnp.abs(x)))


def _sigmoid(x):
    return 0.5 * jnp.tanh(0.5 * x) + 0.5


def _mod_kernel(s_ref, w_ref, b_ref, o_ref):
    o_ref[...] = _dot(s_ref[...].astype(BF16), w_ref[...].astype(BF16)) + b_ref[...]


def _mod_all(s8, mod_w, mod_b):
    tn = 1024
    return pl.pallas_call(
        _mod_kernel,
        grid=(DEPTH, N_MOD * D // tn),
        in_specs=[pl.BlockSpec((8, D), lambda l, j: (0, 0)),
                  pl.BlockSpec((None, D, tn), lambda l, j: (l, 0, j)),
                  pl.BlockSpec((None, 1, tn), lambda l, j: (l, 0, j))],
        out_specs=pl.BlockSpec((None, 8, tn), lambda l, j: (l, 0, j)),
        out_shape=jax.ShapeDtypeStruct((DEPTH, 8, N_MOD * D), F32),
        compiler_params=_cparams(("arbitrary", "arbitrary")),
        name="mod_vectors",
    )(s8, mod_w, mod_b.reshape(DEPTH, 1, N_MOD * D))


def _row_select(m, row0, tm, lat_row, ctx_row):
    rows = row0 + lax.broadcasted_iota(jnp.int32, (tm, 1), 0)
    return jnp.where(rows >= S, m[ctx_row:ctx_row + 1, :], m[lat_row:lat_row + 1, :])


LANES = 128


def _norm_mod_store(x_ref, g_ref, m_ref, xn_ref, inv_ref, row0, tm):
    ones = jnp.ones((D, LANES), BF16)

    def body(c, carry):
        rows = pl.ds(pl.multiple_of(c * ROW_CHUNK, 16), ROW_CHUNK)
        x = x_ref[rows, :]
        ss = _dot((x * x).astype(BF16), ones)
        inv_ref[rows, :] = lax.rsqrt(ss * (1.0 / D) + EPS)
        return carry

    lax.fori_loop(0, tm // ROW_CHUNK, body, 0)

    is_ctx = (row0 + lax.broadcasted_iota(jnp.int32, (tm, 1), 0)) >= S
    inv = inv_ref[...]
    for c in range(D // LANES):
        cs = slice(c * LANES, (c + 1) * LANES)
        g = g_ref[:, cs]
        gain = jnp.where(is_ctx, g * (1.0 + m_ref[4:5, cs]), g * (1.0 + m_ref[1:2, cs]))
        shift = jnp.where(is_ctx, m_ref[3:4, cs], m_ref[0:1, cs])
        xn_ref[:, cs] = (x_ref[:, cs] * inv * gain + shift).astype(BF16)


def _ffn_a_kernel(x_ref, g_ref, m_ref, wg_ref, wu_ref, h_ref, xn_ref, inv_ref, *, tm):
    @pl.when(pl.program_id(1) == 0)
    def _():
        _norm_mod_store(x_ref, g_ref, m_ref, xn_ref, inv_ref, pl.program_id(0) * tm, tm)

    xn = xn_ref[...]
    gate = _dot(xn, wg_ref[...].astype(BF16))
    up = _dot(xn, wu_ref[...].astype(BF16))
    h_ref[...] = (gate * _sigmoid(gate) * up).astype(BF16)


def _ffn_a(x, g, m, ffn_w13, l, a):
    tm, tf = TM, 512
    nf = D_FF // tf
    sub = 3 * l + 2 * a
    return pl.pallas_call(
        functools.partial(_ffn_a_kernel, tm=tm),
        grid=(NT // tm, nf),
        in_specs=[pl.BlockSpec((tm, D), lambda i, j: (i, 0)),
                  pl.BlockSpec((None, 1, D), lambda i, j: (sub, 0, 0)),
                  pl.BlockSpec((None, 8, D), lambda i, j: (sub, 0, 0)),
                  pl.BlockSpec((None, None, D, tf), lambda i, j: (l, a, 0, j)),
                  pl.BlockSpec((None, None, D, tf), lambda i, j: (l, a, 0, j + nf))],
        out_specs=pl.BlockSpec((tm, tf), lambda i, j: (i, j)),
        out_shape=jax.ShapeDtypeStruct((NT, D_FF), BF16),
        scratch_shapes=[pltpu.VMEM((tm, D), BF16), pltpu.VMEM((tm, LANES), F32)],
        compiler_params=_cparams(("arbitrary", "arbitrary")),
        name="ffn_a",
    )(x, g, m, ffn_w13, ffn_w13)


def _ffn_b_kernel(h_ref, w_ref, x_ref, m_ref, o_ref, *, tm, coef):
    acc = _dot(h_ref[...], w_ref[...].astype(BF16))
    gate = _row_select(m_ref[...], pl.program_id(0) * tm, tm, 2, 5)
    o_ref[...] = x_ref[...] + (coef * gate) * acc


def _ffn_b(h, x, m, ffn_w2, l, a):
    tm, tn = TM, 256
    sub = 3 * l + 2 * a
    return pl.pallas_call(
        functools.partial(_ffn_b_kernel, tm=tm, coef=0.5),
        grid=(NT // tm, D // tn),
        in_specs=[pl.BlockSpec((tm, D_FF), lambda i, n: (i, 0)),
                  pl.BlockSpec((None, None, D_FF, tn), lambda i, n: (l, a, 0, n)),
                  pl.BlockSpec((tm, tn), lambda i, n: (i, n)),
                  pl.BlockSpec((None, 8, tn), lambda i, n: (sub, 0, n))],
        out_specs=pl.BlockSpec((tm, tn), lambda i, n: (i, n)),
        out_shape=jax.ShapeDtypeStruct((NT, D), F32),
        compiler_params=_cparams(("arbitrary", "arbitrary")),
        name="ffn_b",
    )(h, ffn_w2, x, m)


def _proj_kernel(x_ref, g_ref, m_ref, w_ref, wt_ref, p_ref, u_ref, lr_ref, xn_ref, inv_ref, *, tm, n_w):
    j = pl.program_id(1)

    @pl.when(j == 0)
    def _():
        _norm_mod_store(x_ref, g_ref, m_ref, xn_ref, inv_ref, pl.program_id(0) * tm, tm)

    @pl.when(j < n_w)
    def _():
        p_ref[...] = _dot(xn_ref[...], w_ref[...].astype(BF16))

    @pl.when(j == n_w)
    def _():
        tail = _dot(xn_ref[...], wt_ref[...].astype(BF16))
        u_ref[...] = tail[:, :DG]
        lr_ref[...] = tail[:, DG:]


def _proj_in(x, g, m, w_in, w_tail, l):
    tm, tn = TM, 512
    n_w = D_IN_MAIN // tn
    return pl.pallas_call(
        functools.partial(_proj_kernel, tm=tm, n_w=n_w),
        grid=(NT // tm, n_w + 1),
        in_specs=[pl.BlockSpec((tm, D), lambda i, j: (i, 0)),
                  pl.BlockSpec((None, 1, D), lambda i, j: (3 * l + 1, 0, 0)),
                  pl.BlockSpec((None, 8, D), lambda i, j: (3 * l + 1, 0, 0)),
                  pl.BlockSpec((None, D, tn), lambda i, j: (l, 0, jnp.minimum(j, n_w - 1))),
                  pl.BlockSpec((None, D, W_TAIL), lambda i, j: (l, 0, 0))],
        out_specs=[pl.BlockSpec((tm, tn), lambda i, j: (i, jnp.minimum(j, n_w - 1))),
                   pl.BlockSpec((tm, DG), lambda i, j: (i, 0)),
                   pl.BlockSpec((tm, LANES), lambda i, j: (i, 0))],
        out_shape=[jax.ShapeDtypeStruct((NT, D_IN_MAIN), F32),
                   jax.ShapeDtypeStruct((NT, DG), F32),
                   jax.ShapeDtypeStruct((NT, LANES), F32)],
        scratch_shapes=[pltpu.VMEM((tm, D), BF16), pltpu.VMEM((tm, LANES), F32)],
        compiler_params=_cparams(("arbitrary", "arbitrary")),
        name="proj_in",
    )(x, g, m, w_in, w_tail)


def _proj_out_kernel(ya_ref, yb_ref, yc_ref, yd_ref, w_ref, x_ref, m_ref, o_ref, *, tm):
    acc = _dot(ya_ref[...], w_ref[0 * DG:1 * DG, :].astype(BF16))
    acc += _dot(yb_ref[...], w_ref[1 * DG:2 * DG, :].astype(BF16))
    acc += _dot(yc_ref[...], w_ref[2 * DG:3 * DG, :].astype(BF16))
    acc += _dot(yd_ref[...], w_ref[3 * DG:4 * DG, :].astype(BF16))
    gate = _row_select(m_ref[...], pl.program_id(0) * tm, tm, 2, 5)
    o_ref[...] = x_ref[...] + gate * acc


def _proj_out(ya, yb, yc, yd, x, m, w_out, l):
    tm, tn = TM, 1024
    yspec = pl.BlockSpec((tm, DG), lambda i, j: (i, 0))
    return pl.pallas_call(
        functools.partial(_proj_out_kernel, tm=tm),
        grid=(NT // tm, D // tn),
        in_specs=[yspec, yspec, yspec, yspec,
                  pl.BlockSpec((None, D, tn), lambda i, j: (l, 0, j)),
                  pl.BlockSpec((tm, tn), lambda i, j: (i, j)),
                  pl.BlockSpec((None, 8, tn), lambda i, j: (3 * l + 1, 0, j))],
        out_specs=pl.BlockSpec((tm, tn), lambda i, j: (i, j)),
        out_shape=jax.ShapeDtypeStruct((NT, D), F32),
        compiler_params=_cparams(("arbitrary", "arbitrary")),
        name="proj_out",
    )(ya, yb, yc, yd, w_out, x, m)


def _final_norm_kernel(x_ref, g_ref, o_ref):
    x = x_ref[...]
    o_ref[...] = x * lax.rsqrt(jnp.mean(x * x, axis=-1, keepdims=True) + EPS) * g_ref[...]


def _final_norm(x, g):
    tm = 1024
    return pl.pallas_call(
        _final_norm_kernel,
        grid=(S // tm,),
        in_specs=[pl.BlockSpec((tm, D), lambda i: (i, 0)),
                  pl.BlockSpec((1, D), lambda i: (0, 0))],
        out_specs=pl.BlockSpec((tm, D), lambda i: (i, 0)),
        out_shape=jax.ShapeDtypeStruct((S, D), F32),
        compiler_params=_cparams(("arbitrary",)),
        name="final_norm",
    )(x, g)


def _rms_rope(t, g, cos, sin_signed, scale):
    y = t * lax.rsqrt(jnp.mean(t * t, axis=-1, keepdims=True) + EPS) * g
    lane = lax.broadcasted_iota(jnp.int32, y.shape, 1)
    partner = jnp.where((lane % 64) < 32, pltpu.roll(y, 96, 1), pltpu.roll(y, 32, 1))
    out = y * cos + partner * sin_signed
    return out * scale if scale != 1.0 else out


LOG2E = 1.4426950408889634


def _attn_prep_kernel(q_ref, k_ref, v_ref, cos_ref, sin_ref, g_ref, qo_ref, kt_ref, vo_ref):
    cos = cos_ref[...]
    sin = sin_ref[...]
    gq = g_ref[0:1, :]
    gk = g_ref[1:2, :]
    q_scale = LOG2E * ATT_HD ** -0.5
    for h in range(4):
        sl = slice(h * ATT_HD, (h + 1) * ATT_HD)
        qo_ref[:, sl] = _rms_rope(q_ref[:, sl], gq, cos, sin, q_scale).astype(BF16)
    for h in range(ATT_KVH):
        sl = slice(h * ATT_HD, (h + 1) * ATT_HD)
        kt_ref[sl, :] = _rms_rope(k_ref[:, sl], gk, cos, sin, 1.0).T.astype(BF16)
    ones = jnp.ones((v_ref.shape[0], ATT_HD), BF16)
    for h in range(ATT_KVH):
        vo_ref[:, 2 * h * ATT_HD:(2 * h + 1) * ATT_HD] = v_ref[:, h * ATT_HD:(h + 1) * ATT_HD].astype(BF16)
        vo_ref[:, (2 * h + 1) * ATT_HD:(2 * h + 2) * ATT_HD] = ones


def _attn_prep(p, cos_t, sin_t, qk_g):
    tm = 768
    return pl.pallas_call(
        _attn_prep_kernel,
        grid=(NT // tm,),
        in_specs=[pl.BlockSpec((tm, 512), lambda i: (i, 2)),
                  pl.BlockSpec((tm, 256), lambda i: (i, 6)),
                  pl.BlockSpec((tm, 256), lambda i: (i, 7)),
                  pl.BlockSpec((tm, ATT_HD), lambda i: (i, 0)),
                  pl.BlockSpec((tm, ATT_HD), lambda i: (i, 0)),
                  pl.BlockSpec((2, ATT_HD), lambda i: (0, 0))],
        out_specs=[pl.BlockSpec((tm, 512), lambda i: (i, 0)),
                   pl.BlockSpec((ATT_KVH * ATT_HD, tm), lambda i: (0, i)),
                   pl.BlockSpec((tm, 2 * ATT_KVH * ATT_HD), lambda i: (i, 0))],
        out_shape=[jax.ShapeDtypeStruct((NT, 512), BF16),
                   jax.ShapeDtypeStruct((ATT_KVH * ATT_HD, NT), BF16),
                   jax.ShapeDtypeStruct((NT, 2 * ATT_KVH * ATT_HD), BF16)],
        compiler_params=_cparams(("arbitrary",)),
        name="attn_prep",
    )(p, p, p, cos_t, sin_t, qk_g)


ATT_TQ = 128
ATT_RB = 32
ATT_CB = 384


def _stack_heads(q_ref):
    return jnp.concatenate([q_ref[:, :ATT_HD], q_ref[:, ATT_HD:]], axis=0)


def _store_heads(o_ref, o):
    half = o.shape[0] // 2
    o_ref[:, :ATT_HD] = o[:half].astype(o_ref.dtype)
    o_ref[:, ATT_HD:] = o[half:].astype(o_ref.dtype)


def _weighted_values(p, v1):
    ov = _dot(p, v1)
    return ov[:, :ATT_HD] / ov[:, ATT_HD:]


def _attn_ctx_kernel(q_ref, kt_ref, v_ref, o_ref):
    s = _dot(_stack_heads(q_ref), kt_ref[...])
    e = jnp.exp2(s - jnp.max(s, axis=-1, keepdims=True))
    _store_heads(o_ref, _weighted_values(e.astype(BF16), v_ref[...]))


def _attn_ctx(qn, kt, vb):
    blk = S // TCX
    return pl.pallas_call(
        _attn_ctx_kernel,
        grid=(ATT_KVH,),
        in_specs=[pl.BlockSpec((TCX, 2 * ATT_HD), lambda h: (blk, h)),
                  pl.BlockSpec((ATT_HD, TCX), lambda h: (h, blk)),
                  pl.BlockSpec((TCX, 2 * ATT_HD), lambda h: (blk, h))],
        out_specs=pl.BlockSpec((TCX, 2 * ATT_HD), lambda h: (0, h)),
        out_shape=jax.ShapeDtypeStruct((TCX, 4 * ATT_HD), BF16),
        compiler_params=_cparams(("arbitrary",)),
        name="attn_ctx",
    )(qn, kt, vb)


def _softmax_rows(s_ref, p_ref, r):
    rows = slice(r * ATT_RB, (r + 1) * ATT_RB)
    ncb = NT // ATT_CB
    mx = s_ref[rows, 0:ATT_CB]
    for c in range(1, ncb):
        mx = jnp.maximum(mx, s_ref[rows, c * ATT_CB:(c + 1) * ATT_CB])
    m = jnp.max(mx, axis=-1, keepdims=True)
    for c in range(ncb):
        cols = slice(c * ATT_CB, (c + 1) * ATT_CB)
        p_ref[rows, cols] = jnp.exp2(s_ref[rows, cols] - m).astype(BF16)


def _attn_full_kernel(q_ref, kt_ref, v_ref, o0_ref, o1_ref, sa_ref, sb_ref, pa_ref, pb_ref):
    @pl.when(pl.program_id(0) == 0)
    def _():
        sb_ref[...] = jnp.zeros_like(sb_ref)

    def stacked_q(h):
        return jnp.concatenate([q_ref[:, 2 * h * ATT_HD:(2 * h + 1) * ATT_HD],
                                q_ref[:, (2 * h + 1) * ATT_HD:(2 * h + 2) * ATT_HD]], axis=0)

    nrb = 2 * ATT_TQ // ATT_RB
    sa_ref[...] = _dot(stacked_q(0), kt_ref[0:ATT_HD, :])
    for r in range(nrb):
        _softmax_rows(sb_ref, pb_ref, r)
    _store_heads(o1_ref, _weighted_values(pb_ref[...], v_ref[:, 2 * ATT_HD:4 * ATT_HD]))
    for r in range(nrb):
        _softmax_rows(sa_ref, pa_ref, r)
    sb_ref[...] = _dot(stacked_q(1), kt_ref[ATT_HD:2 * ATT_HD, :])
    _store_heads(o0_ref, _weighted_values(pa_ref[...], v_ref[:, 0:2 * ATT_HD]))


def _attn_latent(qn, kt, vb):
    nq = S // ATT_TQ
    cur = lambda g: jnp.minimum(g, nq - 1)
    prev = lambda g: jnp.maximum(g - 1, 0)
    resident = dict(pipeline_mode=pl.Buffered(1))
    half = jax.ShapeDtypeStruct((S, 2 * ATT_HD), BF16)
    return pl.pallas_call(
        _attn_full_kernel,
        grid=(nq + 1,),
        in_specs=[pl.BlockSpec((ATT_TQ, 4 * ATT_HD), lambda g: (cur(g), 0)),
                  pl.BlockSpec((ATT_KVH * ATT_HD, NT), lambda g: (0, 0), **resident),
                  pl.BlockSpec((NT, 2 * ATT_KVH * ATT_HD), lambda g: (0, 0), **resident)],
        out_specs=[pl.BlockSpec((ATT_TQ, 2 * ATT_HD), lambda g: (cur(g), 0)),
                   pl.BlockSpec((ATT_TQ, 2 * ATT_HD), lambda g: (prev(g), 0))],
        out_shape=[half, half],
        scratch_shapes=[pltpu.VMEM((2 * ATT_TQ, NT), F32),
                        pltpu.VMEM((2 * ATT_TQ, NT), F32),
                        pltpu.VMEM((2 * ATT_TQ, NT), BF16),
                        pltpu.VMEM((2 * ATT_TQ, NT), BF16)],
        compiler_params=_cparams(("arbitrary",)),
        name="attn_latent",
    )(qn, kt, vb)


def _attn_mixer(p, cos_t, sin_t, qk_g):
    qn, kt, vb = _attn_prep(p, cos_t, sin_t, qk_g)
    y0, y1 = _attn_latent(qn, kt, vb)
    return jnp.concatenate([jnp.concatenate([y0, y1], axis=1), _attn_ctx(qn, kt, vb)], axis=0)


def _walk_block(s, nlat, reverse):
    if reverse:
        return jnp.where(s == 0, nlat, nlat - s)
    return jnp.where(s == 0, nlat, s - 1)


def _seq_edges(s, nlat, reverse):
    if reverse:
        return (s == 0) | (s == nlat), (s == 0) | (s == 1)
    return (s == 0) | (s == 1), (s == 0) | (s == nlat)


def _lru_kernel(xb_ref, xp_ref, xn_ref, cw_ref, cb_ref, wg_ref, bg_ref, lam_ref, *rest,
                nlat, reverse):
    if reverse:
        hf_ref, g_ref, y_ref, xe_ref, carry_ref = rest
    else:
        h_ref, xe_ref, carry_ref = rest
    s = pl.program_id(0)
    starts, ends = _seq_edges(s, nlat, reverse)

    @pl.when(s == 0)
    def _():
        carry_ref[...] = jnp.zeros_like(carry_ref)

    xe_ref[0:8, :] = jnp.where(starts, 0.0, xp_ref[...])
    xe_ref[8:8 + TB, :] = xb_ref[...]
    xe_ref[8 + TB:16 + TB, :] = jnp.where(ends, 0.0, xn_ref[...])
    u = cb_ref[...]
    for j in range(4):
        u = u + cw_ref[j:j + 1, :] * xe_ref[6 + j:6 + j + TB, :]

    gates = _dot(u.astype(BF16), wg_ref[...]) + bg_ref[...]
    r = _sigmoid(gates[:, :DG])
    i = _sigmoid(gates[:, DG:])
    a = jnp.exp((-LRU_C) * r * _softplus(-lam_ref[...]))
    b = jnp.sqrt(1.0 - a * a) * (i * u)

    row8 = lax.broadcasted_iota(jnp.int32, (TB, 1), 0) % 8
    for sh in (1, 2, 4):
        if reverse:
            a_s, b_s, valid = pltpu.roll(a, TB - sh, 0), pltpu.roll(b, TB - sh, 0), row8 < 8 - sh
        else:
            a_s, b_s, valid = pltpu.roll(a, sh, 0), pltpu.roll(b, sh, 0), row8 >= sh
        b = b + jnp.where(valid, a * b_s, 0.0)
        a = jnp.where(valid, a * a_s, a)
    carry = carry_ref[0:1, :]
    groups = [None] * (TB // 8)
    for gi in (range(TB // 8 - 1, -1, -1) if reverse else range(TB // 8)):
        hg = b[8 * gi:8 * gi + 8, :] + a[8 * gi:8 * gi + 8, :] * carry
        groups[gi] = hg
        carry = hg[0:1, :] if reverse else hg[7:8, :]
    h = jnp.concatenate(groups, axis=0)
    carry_ref[0:1, :] = carry

    if reverse:
        g = g_ref[...]
        gelu = 0.5 * g * (1.0 + jnp.tanh(0.7978845608028654 * (g + 0.044715 * (g * g * g))))
        y_ref[...] = ((hf_ref[...] + h) * gelu).astype(BF16)
    else:
        h_ref[...] = h


def _lru_pass(p, conv_w, conv_b, wg, bg, lam, nlat, reverse, hf=None):
    blk = lambda s: _walk_block(s, nlat, reverse)
    nrow8 = (nlat + 1) * TB // 8
    in_specs = [pl.BlockSpec((TB, DG), lambda s: (blk(s), 0)),
                pl.BlockSpec((8, DG), lambda s: (jnp.maximum(blk(s) * (TB // 8) - 1, 0), 0)),
                pl.BlockSpec((8, DG), lambda s: (jnp.minimum((blk(s) + 1) * (TB // 8), nrow8 - 1), 0)),
                pl.BlockSpec((4, DG), lambda s: (0, 0)),
                pl.BlockSpec((1, DG), lambda s: (0, 0)),
                pl.BlockSpec((DG, 2 * DG), lambda s: (0, 0)),
                pl.BlockSpec((1, 2 * DG), lambda s: (0, 0)),
                pl.BlockSpec((1, DG), lambda s: (0, 0))]
    args = [p, p, p, conv_w, conv_b, wg, bg, lam]
    if reverse:
        in_specs += [pl.BlockSpec((TB, DG), lambda s: (blk(s), 0)),
                     pl.BlockSpec((TB, DG), lambda s: (blk(s), 1))]
        args += [hf, p]
    return pl.pallas_call(
        functools.partial(_lru_kernel, nlat=nlat, reverse=reverse),
        grid=(nlat + 1,),
        in_specs=in_specs,
        out_specs=pl.BlockSpec((TB, DG), lambda s: (blk(s), 0)),
        out_shape=jax.ShapeDtypeStruct(((nlat + 1) * TB, DG), BF16 if reverse else F32),
        scratch_shapes=[pltpu.VMEM((TB + 16, DG), F32), pltpu.VMEM((8, DG), F32)],
        compiler_params=_cparams(("arbitrary",)),
        name="lru_bwd" if reverse else "lru_fwd",
    )(*args)


def _lru_gate_dense(w):
    eye = jnp.eye(LRU_HEADS, dtype=w.dtype)
    dense = jnp.einsum('ghij,hk->ghikj', w, eye).reshape(2, DG, DG)
    return jnp.concatenate([dense[0], dense[1]], axis=1).astype(BF16)


def _lru_mixer(p, conv_w, conv_b, gate_w, gate_b, lam, nlat=S // TB):
    cb = conv_b[None, :]
    hf = _lru_pass(p, conv_w, cb, _lru_gate_dense(gate_w[0]), gate_b[0].reshape(1, 2 * DG),
                   lam[0:1], nlat, False)
    return _lru_pass(p, conv_w, cb, _lru_gate_dense(gate_w[1]), gate_b[1].reshape(1, 2 * DG),
                     lam[1:2], nlat, True, hf=hf)


def _split3(x):
    hi = x.astype(BF16)
    r1 = x - hi.astype(F32)
    mid = r1.astype(BF16)
    lo = (r1 - mid.astype(F32)).astype(BF16)
    return hi, mid, lo


def _gla_kernel(q_ref, k_ref, v_ref, lr_ref, wl_ref, bl_ref, *rest, reverse):
    if reverse:
        of_ref, g_ref, og_ref, y_ref, st_ref, o_acc = rest
    else:
        o_acc, st_ref = rest
    s = pl.program_id(0)
    nck = TB // GLA_CHUNK
    hw = GLA_HEADS * GLA_DK

    @pl.when(s == 0)
    def _():
        st_ref[...] = jnp.zeros_like(st_ref)

    x = _dot(lr_ref[...].astype(BF16), wl_ref[...]) + bl_ref[...]
    log_a = (jnp.minimum(x, 0.0) - jnp.log(1.0 + jnp.exp(-jnp.abs(x)))) * (1.0 / GLA_TAU)

    ri = lax.broadcasted_iota(jnp.int32, (TB, TB), 0)
    ci = lax.broadcasted_iota(jnp.int32, (TB, TB), 1)
    same = (ri // GLA_CHUNK) == (ci // GLA_CHUNK)
    causal = same & ((ci >= ri) if reverse else (ci <= ri))
    tri = jnp.where(causal, 1.0, 0.0).astype(BF16)
    ones = jnp.where(same, 1.0, 0.0).astype(BF16)
    parts = _split3(log_a)
    bcum = _dot(tri, parts[0]) + _dot(tri, parts[1]) + _dot(tri, parts[2])
    tot = _dot(ones, parts[0]) + _dot(ones, parts[1]) + _dot(ones, parts[2])

    q_t = q_ref[...] * (GLA_DK ** -0.5) * jnp.exp(bcum)
    k_in = (k_ref[...] * jnp.exp(-bcum)).astype(BF16)
    k_out = (k_ref[...] * jnp.exp(tot - bcum)).astype(BF16)
    decay = jnp.exp(tot)
    lane = lax.broadcasted_iota(jnp.int32, (1, hw), 1)

    st = st_ref[...]
    for h in range(GLA_HEADS):
        head = (lane // GLA_DK) == h
        qh = jnp.where(head, q_t, 0.0).astype(BF16)
        vh = v_ref[:, h * GLA_DV:(h + 1) * GLA_DV].astype(BF16)
        att = jnp.where(causal, _dot_nt(qh, k_in), 0.0)
        o_acc[:, h * GLA_DV:(h + 1) * GLA_DV] = _dot(att.astype(BF16), vh)
    for c in (range(nck - 1, -1, -1) if reverse else range(nck)):
        rows = slice(c * GLA_CHUNK, (c + 1) * GLA_CHUNK)
        new_st = []
        for h in range(GLA_HEADS):
            head = (lane // GLA_DK) == h
            st_h = st[h * GLA_DV:(h + 1) * GLA_DV, :]
            qh = jnp.where(head, q_t[rows, :], 0.0).astype(BF16)
            cols = slice(h * GLA_DV, (h + 1) * GLA_DV)
            o_acc[rows, cols] += _dot_nt(qh, st_h.astype(BF16))
            upd = _dot_tn(v_ref[rows, cols].astype(BF16), k_out[rows, :])
            new_st.append(decay[c * GLA_CHUNK:c * GLA_CHUNK + 1, :] * st_h + jnp.where(head, upd, 0.0))
        st = jnp.concatenate(new_st, axis=0)
    st_ref[...] = st

    if reverse:
        g = g_ref[...]
        og = og_ref[...]
        for h in range(GLA_HEADS):
            cols = slice(h * GLA_DV, (h + 1) * GLA_DV)
            o = of_ref[:, cols] + o_acc[:, cols]
            o = o * lax.rsqrt(jnp.mean(o * o, axis=-1, keepdims=True) + EPS) * og
            gh = g[:, cols]
            y_ref[:, cols] = (o * (gh * _sigmoid(gh))).astype(BF16)


def _gla_pass(p, lr, wl, bl, nlat, reverse, o_f=None, out_g=None):
    blk = lambda s: _walk_block(s, nlat, reverse)
    in_specs = [pl.BlockSpec((TB, 256), lambda s: (blk(s), 8)),
                pl.BlockSpec((TB, 256), lambda s: (blk(s), 9)),
                pl.BlockSpec((TB, DG), lambda s: (blk(s), 5)),
                pl.BlockSpec((TB, 128), lambda s: (blk(s), 0)),
                pl.BlockSpec((128, 256), lambda s: (0, 0)),
                pl.BlockSpec((1, 256), lambda s: (0, 0))]
    args = [p, p, p, lr, wl, bl]
    scratch = [pltpu.VMEM((GLA_HEADS * GLA_DV, GLA_HEADS * GLA_DK), F32)]
    if reverse:
        in_specs += [pl.BlockSpec((TB, DG), lambda s: (blk(s), 0)),
                     pl.BlockSpec((TB, DG), lambda s: (blk(s), 6)),
                     pl.BlockSpec((1, GLA_DV), lambda s: (0, 0))]
        args += [o_f, p, out_g]
        scratch = scratch + [pltpu.VMEM((TB, DG), F32)]
    return pl.pallas_call(
        functools.partial(_gla_kernel, reverse=reverse),
        grid=(nlat + 1,),
        in_specs=in_specs,
        out_specs=pl.BlockSpec((TB, DG), lambda s: (blk(s), 0)),
        out_shape=jax.ShapeDtypeStruct(((nlat + 1) * TB, DG), BF16 if reverse else F32),
        scratch_shapes=scratch,
        compiler_params=_cparams(("arbitrary",)),
        name="gla_bwd" if reverse else "gla_fwd",
    )(*args)


def _gla_gate_dense(gate_w, d):
    w = jnp.zeros((128, GLA_HEADS * GLA_DK), F32)
    return w.at[d * GLA_RANK:(d + 1) * GLA_RANK].set(gate_w[d]).astype(BF16)


def _gla_mixer(p, lr, gate_w, gate_b, out_g, nlat=S // TB):
    o_f = _gla_pass(p, lr, _gla_gate_dense(gate_w, 0), gate_b[0:1], nlat, False)
    return _gla_pass(p, lr, _gla_gate_dense(gate_w, 1), gate_b[1:2], nlat, True,
                     o_f=o_f, out_g=out_g[None, :])


FN1 = 64
FN2 = 128
FN2_TILE = 8
FK1_TILE = 8


def _dft_consts():
    def cs(n, rows=None, cols=None):
        a = 2.0 * np.pi * np.outer(np.arange(n if rows is None else rows),
                                   np.arange(n if cols is None else cols)) / n
        return np.cos(a), np.sin(a)

    c1, s1 = cs(FN1)
    m1 = np.concatenate([c1, -s1], axis=0)
    ang = 2.0 * np.pi * np.outer(np.arange(FN1), np.arange(FN2)) / (FN1 * FN2)
    tw_c = np.cos(ang).reshape(FN1, FN2 // FN2_TILE, FN2_TILE).transpose(1, 0, 2)
    tw_s = np.sin(ang).reshape(FN1, FN2 // FN2_TILE, FN2_TILE).transpose(1, 0, 2)
    c2, s2 = cs(FN2)
    m2 = np.block([[c2, s2], [-s2, c2]])
    cc, sc = cs(FNET_CH)
    eye = np.eye(DG // FNET_CH)
    ccb, scb = np.kron(eye, cc), np.kron(eye, sc)
    ch_lat = np.concatenate([ccb, scb], axis=0) / np.sqrt(FN1 * FN2 * FNET_CH)
    ch_ctx = np.concatenate([ccb, -scb], axis=1)
    ct, st = cs(TCX)
    t_ctx = np.concatenate([ct, st], axis=1) / np.sqrt(TCX * FNET_CH)
    bf = lambda a: jnp.asarray(a, F32).astype(BF16)
    return dict(m1=bf(m1), tw_c=jnp.asarray(tw_c, F32), tw_s=jnp.asarray(tw_s, F32),
                m2=bf(m2), ch_lat=bf(ch_lat), ch_ctx=bf(ch_ctx), t_ctx=bf(t_ctx))


def _fnet1_kernel(u_ref, m1_ref, tc_ref, ts_ref, z_ref):
    a = _dot(m1_ref[...], u_ref[...].astype(BF16))
    ar, ai = a[:FN1], a[FN1:]
    for n in range(FN2_TILE):
        sl = slice(n * DG, (n + 1) * DG)
        c = tc_ref[:, n:n + 1]
        sn = ts_ref[:, n:n + 1]
        z_ref[0, :, sl] = (ar[:, sl] * c + ai[:, sl] * sn).astype(BF16)
        z_ref[1, :, sl] = (ai[:, sl] * c - ar[:, sl] * sn).astype(BF16)


def _fnet2_kernel(z_ref, m2_ref, ch_ref, y_ref):
    for i in range(FK1_TILE):
        z = jnp.concatenate([z_ref[0, i], z_ref[1, i]], axis=0)
        y3 = _dot(m2_ref[...], z)
        lhs = jnp.concatenate([y3[:FN2], y3[FN2:]], axis=1).astype(BF16)
        y_ref[:, i * DG:(i + 1) * DG] = _dot(lhs, ch_ref[...]).astype(BF16)


def _fnet_ctx_kernel(u_ref, ch_ref, t_ref, y_ref):
    v =_dot(u_ref[...].astype(BF16), ch_ref[...])
    rhs = jnp.concatenate([v[:, :DG], v[:, DG:]], axis=0).astype(BF16)
    y_ref[...] = _dot(t_ref[...], rhs).astype(BF16)


def _fnet_mixer(u, consts):
    cols = FN2_TILE * DG
    z = pl.pallas_call(
        _fnet1_kernel,
        grid=(FN2 // FN2_TILE,),
        in_specs=[pl.BlockSpec((FN1, cols), lambda j: (0, j)),
                  pl.BlockSpec((2 * FN1, FN1), lambda j: (0, 0)),
                  pl.BlockSpec((None, FN1, FN2_TILE), lambda j: (j, 0, 0)),
                  pl.BlockSpec((None, FN1, FN2_TILE), lambda j: (j, 0, 0))],
        out_specs=pl.BlockSpec((2, FN1, cols), lambda j: (0, 0, j)),
        out_shape=jax.ShapeDtypeStruct((2, FN1, FN2 * DG), BF16),
        compiler_params=_cparams(("arbitrary",)),
        name="fnet_stage1",
    )(u.reshape(NT // FN2, FN2 * DG), consts['m1'], consts['tw_c'], consts['tw_s'])
    y = pl.pallas_call(
        _fnet2_kernel,
        grid=(FN1 // FK1_TILE,),
        in_specs=[pl.BlockSpec((2, FK1_TILE, FN2, DG), lambda j: (0, j, 0, 0)),
                  pl.BlockSpec((2 * FN2, 2 * FN2), lambda j: (0, 0)),
                  pl.BlockSpec((2 * DG, DG), lambda j: (0, 0))],
        out_specs=pl.BlockSpec((FN2, FK1_TILE * DG), lambda j: (0, j)),
        out_shape=jax.ShapeDtypeStruct((FN2, FN1 * DG), BF16),
        compiler_params=_cparams(("arbitrary",)),
        name="fnet_stage2",
    )(z.reshape(2, FN1, FN2, DG), consts['m2'], consts['ch_lat'])
    y_ctx = pl.pallas_call(
        _fnet_ctx_kernel,
        grid=(1,),
        in_specs=[pl.BlockSpec((TCX, DG), lambda j: (S // TCX, 0)),
                  pl.BlockSpec((DG, 2 * DG), lambda j: (0, 0)),
                  pl.BlockSpec((TCX, 2 * TCX), lambda j: (0, 0))],
        out_specs=pl.BlockSpec((TCX, DG), lambda j: (0, 0)),
        out_shape=jax.ShapeDtypeStruct((TCX, DG), BF16),
        compiler_params=_cparams(("arbitrary",)),
        name="fnet_ctx",
    )(u, consts['ch_ctx'], consts['t_ctx'])
    return jnp.concatenate([y.reshape(S, DG), y_ctx], axis=0)


def _rope_tables():
    t = np.arange(S)
    half = 32
    freqs = ROPE_THETA ** (-np.arange(half, dtype=np.float64) / half)
    ang_r = (t // GRID_W)[:, None] * freqs
    ang_c = (t % GRID_W)[:, None] * freqs
    cos = np.concatenate([np.cos(ang_r), np.cos(ang_r), np.cos(ang_c), np.cos(ang_c)], axis=1)
    sin = np.concatenate([-np.sin(ang_r), np.sin(ang_r), -np.sin(ang_c), np.sin(ang_c)], axis=1)
    cos = np.concatenate([cos, np.ones((TCX, ATT_HD))], axis=0)
    sin = np.concatenate([sin, np.zeros((TCX, ATT_HD))], axis=0)
    return jnp.asarray(cos, F32), jnp.asarray(sin, F32)


def _mod_rows(mods):
    m = mods[:, 0:2].reshape(DEPTH, 2, 3, 3, D).transpose(0, 2, 1, 3, 4).reshape(3 * DEPTH, 6, D)
    return jnp.pad(m, ((0, 0), (0, 2), (0, 0)))


def kernel(x, c, ctx, c_ctx, mod_w, mod_b, norm_g, final_g, ffn_w13, ffn_w2, w_in, w_out,
           lru_conv_w, lru_conv_b, lru_gate_w, lru_gate_b, lru_lambda, attn_qk_g,
           gla_gate_w, gla_gate_b, gla_out_g):
    xs = jnp.concatenate([x[0], ctx[0]], axis=0)
    s8 = jnp.concatenate([jax.nn.silu(c), jax.nn.silu(c_ctx)[None, :],
                          jnp.zeros((6, D), F32)], axis=0)
    m = _mod_rows(_mod_all(s8, mod_w, mod_b))
    g = norm_g.reshape(3 * DEPTH, 1, D)
    w_tail = jnp.concatenate([w_in[:, :, D_IN_MAIN + 2 * GLA_RANK:],
                              w_in[:, :, D_IN_MAIN:D_IN_MAIN + 2 * GLA_RANK],
                              jnp.zeros((DEPTH, D, W_TAIL - DG - 2 * GLA_RANK), F32)], axis=2)
    cos_t, sin_t = _rope_tables()
    consts = _dft_consts()

    for l in range(DEPTH):
        h = _ffn_a(xs, g, m, ffn_w13, l, 0)
        xs = _ffn_b(h, xs, m, ffn_w2, l, 0)
        p, u, lr = _proj_in(xs, g, m, w_in, w_tail, l)
        ya = _lru_mixer(p, lru_conv_w[l], lru_conv_b[l], lru_gate_w[l], lru_gate_b[l], lru_lambda[l])
        yb = _attn_mixer(p, cos_t, sin_t, attn_qk_g[l])
        yc = _gla_mixer(p, lr, gla_gate_w[l], gla_gate_b[l], gla_out_g[l])
        yd = _fnet_mixer(u, consts)
        xs = _proj_out(ya, yb, yc, yd, xs, m, w_out, l)
        h = _ffn_a(xs, g, m, ffn_w13, l, 1)
        xs = _ffn_b(h, xs, m, ffn_w2, l, 1)
    return _final_norm(xs, final_g[None, :])[None]
```

```python
import functools

import numpy as np
import jax
import jax.numpy as jnp
from jax import lax
from jax.experimental import pallas as pl
from jax.experimental.pallas import tpu as pltpu

F32 = jnp.float32
BF16 = jnp.bfloat16

D = 2048
S = 8192
TCX = 256
NT = S + TCX
DEPTH = 4
GRID_W = 64
EPS = 1e-6
D_FF = 5632
N_MOD = 9
DG = 512
LRU_HEADS = 8
LRU_HD = 64
LRU_C = 8.0
ATT_HD = 128
ATT_KVH = 2
ROPE_THETA = 10000.0
GLA_HEADS = 4
GLA_DK = 64
GLA_DV = 128
GLA_RANK = 16
GLA_TAU = 16.0
GLA_CHUNK = 64
FNET_CH = 128

D_IN_MAIN = 3584
W_TAIL = 640

V7X_VMEM_LIMIT_BYTES = 56 * 1024 * 1024
TM = 1056
TB = 256
ROW_CHUNK = 176


def _cparams(sem):
    return pltpu.CompilerParams(dimension_semantics=sem, vmem_limit_bytes=V7X_VMEM_LIMIT_BYTES)


def _dot(a, b):
    return jnp.dot(a, b, preferred_element_type=F32)


def _dot_nt(a, b):
    return lax.dot_general(a, b, (((1,), (1,)), ((), ())), preferred_element_type=F32)


def _dot_tn(a, b):
    return lax.dot_general(a, b, (((0,), (0,)), ((), ())), preferred_element_type=F32)


def _softplus(x):
    return jnp.maximum(x, 0.0) + jnp.log(1.0 + jnp.exp(-jnp.abs(x)))


def _sigmoid(x):
    return 0.5 * jnp.tanh(0.5 * x) + 0.5


def _mod_kernel(s_ref, w_ref, b_ref, o_ref):
    o_ref[...] = _dot(s_ref[...].astype(BF16), w_ref[...].astype(BF16)) + b_ref[...]


def _mod_all(s8, mod_w, mod_b):
    tn = 1024
    return pl.pallas_call(
        _mod_kernel,
        grid=(DEPTH, N_MOD * D // tn),
        in_specs=[pl.BlockSpec((8, D), lambda l, j: (0, 0)),
                  pl.BlockSpec((None, D, tn), lambda l, j: (l, 0, j)),
                  pl.BlockSpec((None, 1, tn), lambda l, j: (l, 0, j))],
        out_specs=pl.BlockSpec((None, 8, tn), lambda l, j: (l, 0, j)),
        out_shape=jax.ShapeDtypeStruct((DEPTH, 8, N_MOD * D), F32),
        compiler_params=_cparams(("arbitrary", "arbitrary")),
        name="mod_vectors",
    )(s8, mod_w, mod_b.reshape(DEPTH, 1, N_MOD * D))


def _row_select(m, row0, tm, lat_row, ctx_row):
    rows = row0 + lax.broadcasted_iota(jnp.int32, (tm, 1), 0)
    return jnp.where(rows >= S, m[ctx_row:ctx_row + 1, :], m[lat_row:lat_row + 1, :])


LANES = 128


def _norm_mod_store(x_ref, g_ref, m_ref, xn_ref, inv_ref, row0, tm):
    ones = jnp.ones((D, LANES), BF16)

    def body(c, carry):
        rows = pl.ds(pl.multiple_of(c * ROW_CHUNK, 16), ROW_CHUNK)
        x = x_ref[rows, :]
        ss = _dot((x * x).astype(BF16), ones)
        inv_ref[rows, :] = lax.rsqrt(ss * (1.0 / D) + EPS)
        return carry

    lax.fori_loop(0, tm // ROW_CHUNK, body, 0)

    is_ctx = (row0 + lax.broadcasted_iota(jnp.int32, (tm, 1), 0)) >= S
    inv = inv_ref[...]
    for c in range(D // LANES):
        cs = slice(c * LANES, (c + 1) * LANES)
        g = g_ref[:, cs]
        gain = jnp.where(is_ctx, g * (1.0 + m_ref[4:5, cs]), g * (1.0 + m_ref[1:2, cs]))
        shift = jnp.where(is_ctx, m_ref[3:4, cs], m_ref[0:1, cs])
        xn_ref[:, cs] = (x_ref[:, cs] * inv * gain + shift).astype(BF16)


def _ffn_a_kernel(x_ref, g_ref, m_ref, wg_ref, wu_ref, h_ref, xn_ref, inv_ref, *, tm):
    @pl.when(pl.program_id(1) == 0)
    def _():
        _norm_mod_store(x_ref, g_ref, m_ref, xn_ref, inv_ref, pl.program_id(0) * tm, tm)

    xn = xn_ref[...]
    gate = _dot(xn, wg_ref[...].astype(BF16))
    up = _dot(xn, wu_ref[...].astype(BF16))
    h_ref[...] = (gate * _sigmoid(gate) * up).astype(BF16)


def _ffn_a(x, g, m, ffn_w13, l, a):
    tm, tf = TM, 512
    nf = D_FF // tf
    sub = 3 * l + 2 * a
    return pl.pallas_call(
        functools.partial(_ffn_a_kernel, tm=tm),
        grid=(NT // tm, nf),
        in_specs=[pl.BlockSpec((tm, D), lambda i, j: (i, 0)),
                  pl.BlockSpec((None, 1, D), lambda i, j: (sub, 0, 0)),
                  pl.BlockSpec((None, 8, D), lambda i, j: (sub, 0, 0)),
                  pl.BlockSpec((None, None, D, tf), lambda i, j: (l, a, 0, j)),
                  pl.BlockSpec((None, None, D, tf), lambda i, j: (l, a, 0, j + nf))],
        out_specs=pl.BlockSpec((tm, tf), lambda i, j: (i, j)),
        out_shape=jax.ShapeDtypeStruct((NT, D_FF), BF16),
        scratch_shapes=[pltpu.VMEM((tm, D), BF16), pltpu.VMEM((tm, LANES), F32)],
        compiler_params=_cparams(("arbitrary", "arbitrary")),
        name="ffn_a",
    )(x, g, m, ffn_w13, ffn_w13)


def _ffn_b_kernel(h_ref, w_ref, x_ref, m_ref, o_ref, *, tm, coef):
    acc = _dot(h_ref[...], w_ref[...].astype(BF16))
    gate = _row_select(m_ref[...], pl.program_id(0) * tm, tm, 2, 5)
    o_ref[...] = x_ref[...] + (coef * gate) * acc


def _ffn_b(h, x, m, ffn_w2, l, a):
    tm, tn = TM, 512
    sub = 3 * l + 2 * a
    return pl.pallas_call(
        functools.partial(_ffn_b_kernel, tm=tm, coef=0.5),
        grid=(NT // tm, D // tn),
        in_specs=[pl.BlockSpec((tm, D_FF), lambda i, n: (i, 0), pipeline_mode=pl.Buffered(1)),
                  pl.BlockSpec((None, None, D_FF, tn), lambda i, n: (l, a, 0, n)),
                  pl.BlockSpec((tm, tn), lambda i, n: (i, n)),
                  pl.BlockSpec((None, 8, tn), lambda i, n: (sub, 0, n))],
        out_specs=pl.BlockSpec((tm, tn), lambda i, n: (i, n)),
        out_shape=jax.ShapeDtypeStruct((NT, D), F32),
        compiler_params=_cparams(("arbitrary", "arbitrary")),
        name="ffn_b",
    )(h, ffn_w2, x, m)


def _proj_kernel(x_ref, g_ref, m_ref, w_ref, wt_ref, p_ref, u_ref, lr_ref, xn_ref, inv_ref, *, tm, n_w):
    j = pl.program_id(1)

    @pl.when(j == 0)
    def _():
        _norm_mod_store(x_ref, g_ref, m_ref, xn_ref, inv_ref, pl.program_id(0) * tm, tm)

    @pl.when(j < n_w)
    def _():
        p_ref[...] = _dot(xn_ref[...], w_ref[...].astype(BF16))

    @pl.when(j == n_w)
    def _():
        tail = _dot(xn_ref[...], wt_ref[...].astype(BF16))
        u_ref[...] = tail[:, :DG]
        lr_ref[...] = tail[:, DG:]


def _proj_in(x, g, m, w_in, w_tail, l):
    tm, tn = TM, 512
    n_w = D_IN_MAIN // tn
    return pl.pallas_call(
        functools.partial(_proj_kernel, tm=tm, n_w=n_w),
        grid=(NT // tm, n_w + 1),
        in_specs=[pl.BlockSpec((tm, D), lambda i, j: (i, 0)),
                  pl.BlockSpec((None, 1, D), lambda i, j: (3 * l + 1, 0, 0)),
                  pl.BlockSpec((None, 8, D), lambda i, j: (3 * l + 1, 0, 0)),
                  pl.BlockSpec((None, D, tn), lambda i, j: (l, 0, jnp.minimum(j, n_w - 1))),
                  pl.BlockSpec((None, D, W_TAIL), lambda i, j: (l, 0, 0))],
        out_specs=[pl.BlockSpec((tm, tn), lambda i, j: (i, jnp.minimum(j, n_w - 1))),
                   pl.BlockSpec((tm, DG), lambda i, j: (i, 0)),
                   pl.BlockSpec((tm, LANES), lambda i, j: (i, 0))],
        out_shape=[jax.ShapeDtypeStruct((NT, D_IN_MAIN), F32),
                   jax.ShapeDtypeStruct((NT, DG), F32),
                   jax.ShapeDtypeStruct((NT, LANES), F32)],
        scratch_shapes=[pltpu.VMEM((tm, D), BF16), pltpu.VMEM((tm, LANES), F32)],
        compiler_params=_cparams(("arbitrary", "arbitrary")),
        name="proj_in",
    )(x, g, m, w_in, w_tail)


def _proj_out_kernel(ya_ref, yb_ref, yc_ref, yd_ref, w_ref, x_ref, m_ref, o_ref, *, tm):
    acc = _dot(ya_ref[...], w_ref[0 * DG:1 * DG, :].astype(BF16))
    acc += _dot(yb_ref[...], w_ref[1 * DG:2 * DG, :].astype(BF16))
    acc += _dot(yc_ref[...], w_ref[2 * DG:3 * DG, :].astype(BF16))
    acc += _dot(yd_ref[...], w_ref[3 * DG:4 * DG, :].astype(BF16))
    gate = _row_select(m_ref[...], pl.program_id(0) * tm, tm, 2, 5)
    o_ref[...] = x_ref[...] + gate * acc


def _proj_out(ya, yb, yc, yd, x, m, w_out, l):
    tm, tn = TM, 1024
    yspec = pl.BlockSpec((tm, DG), lambda i, j: (i, 0))
    return pl.pallas_call(
        functools.partial(_proj_out_kernel, tm=tm),
        grid=(NT // tm, D // tn),
        in_specs=[yspec, yspec, yspec, yspec,
                  pl.BlockSpec((None, D, tn), lambda i, j: (l, 0, j)),
                  pl.BlockSpec((tm, tn), lambda i, j: (i, j)),
                  pl.BlockSpec((None, 8, tn), lambda i, j: (3 * l + 1, 0, j))],
        out_specs=pl.BlockSpec((tm, tn), lambda i, j: (i, j)),
        out_shape=jax.ShapeDtypeStruct((NT, D), F32),
        compiler_params=_cparams(("arbitrary", "arbitrary")),
        name="proj_out",
    )(ya, yb, yc, yd, w_out, x, m)


def _final_norm_kernel(x_ref, g_ref, o_ref):
    x = x_ref[...]
    o_ref[...] = x * lax.rsqrt(jnp.mean(x * x, axis=-1, keepdims=True) + EPS) * g_ref[...]


def _final_norm(x, g):
    tm = 1024
    return pl.pallas_call(
        _final_norm_kernel,
        grid=(S // tm,),
        in_specs=[pl.BlockSpec((tm, D), lambda i: (i, 0)),
                  pl.BlockSpec((1, D), lambda i: (0, 0))],
        out_specs=pl.BlockSpec((tm, D), lambda i: (i, 0)),
        out_shape=jax.ShapeDtypeStruct((S, D), F32),
        compiler_params=_cparams(("arbitrary",)),
        name="final_norm",
    )(x, g)


def _rms_rope(t, g, cos, sin_signed, scale):
    y = t * lax.rsqrt(jnp.mean(t * t, axis=-1, keepdims=True) + EPS) * g
    lane = lax.broadcasted_iota(jnp.int32, y.shape, 1)
    partner = jnp.where((lane % 64) < 32, pltpu.roll(y, 96, 1), pltpu.roll(y, 32, 1))
    out = y * cos + partner * sin_signed
    return out * scale if scale != 1.0 else out


LOG2E = 1.4426950408889634


def _attn_prep_kernel(q_ref, k_ref, v_ref, cos_ref, sin_ref, g_ref, qo_ref, kt_ref, vo_ref):
    cos = cos_ref[...]
    sin = sin_ref[...]
    gq = g_ref[0:1, :]
    gk = g_ref[1:2, :]
    q_scale = LOG2E * ATT_HD ** -0.5
    for h in range(4):
        sl = slice(h * ATT_HD, (h + 1) * ATT_HD)
        qo_ref[:, sl] = _rms_rope(q_ref[:, sl], gq, cos, sin, q_scale).astype(BF16)
    for h in range(ATT_KVH):
        sl = slice(h * ATT_HD, (h + 1) * ATT_HD)
        kt_ref[sl, :] = _rms_rope(k_ref[:, sl], gk, cos, sin, 1.0).T.astype(BF16)
    ones = jnp.ones((v_ref.shape[0], ATT_HD), BF16)
    for h in range(ATT_KVH):
        vo_ref[:, 2 * h * ATT_HD:(2 * h + 1) * ATT_HD] = v_ref[:, h * ATT_HD:(h + 1) * ATT_HD].astype(BF16)
        vo_ref[:, (2 * h + 1) * ATT_HD:(2 * h + 2) * ATT_HD] = ones


def _attn_prep(p, cos_t, sin_t, qk_g):
    tm = 768
    return pl.pallas_call(
        _attn_prep_kernel,
        grid=(NT // tm,),
        in_specs=[pl.BlockSpec((tm, 512), lambda i: (i, 2)),
                  pl.BlockSpec((tm, 256), lambda i: (i, 6)),
                  pl.BlockSpec((tm, 256), lambda i: (i, 7)),
                  pl.BlockSpec((tm, ATT_HD), lambda i: (i, 0)),
                  pl.BlockSpec((tm, ATT_HD), lambda i: (i, 0)),
                  pl.BlockSpec((2, ATT_HD), lambda i: (0, 0))],
        out_specs=[pl.BlockSpec((tm, 512), lambda i: (i, 0)),
                   pl.BlockSpec((ATT_KVH * ATT_HD, tm), lambda i: (0, i)),
                   pl.BlockSpec((tm, 2 * ATT_KVH * ATT_HD), lambda i: (i, 0))],
        out_shape=[jax.ShapeDtypeStruct((NT, 512), BF16),
                   jax.ShapeDtypeStruct((ATT_KVH * ATT_HD, NT), BF16),
                   jax.ShapeDtypeStruct((NT, 2 * ATT_KVH * ATT_HD), BF16)],
        compiler_params=_cparams(("arbitrary",)),
        name="attn_prep",
    )(p, p, p, cos_t, sin_t, qk_g)


ATT_TQ = 128
ATT_RB = 32
ATT_CB = 384


def _stack_heads(q_ref):
    return jnp.concatenate([q_ref[:, :ATT_HD], q_ref[:, ATT_HD:]], axis=0)


def _store_heads(o_ref, o):
    half = o.shape[0] // 2
    o_ref[:, :ATT_HD] = o[:half].astype(o_ref.dtype)
    o_ref[:, ATT_HD:] = o[half:].astype(o_ref.dtype)


def _weighted_values(p, v1):
    ov = _dot(p, v1)
    return ov[:, :ATT_HD] / ov[:, ATT_HD:]


def _attn_ctx_kernel(q_ref, kt_ref, v_ref, o_ref):
    s = _dot(_stack_heads(q_ref), kt_ref[...])
    e = jnp.exp2(s - jnp.max(s, axis=-1, keepdims=True))
    _store_heads(o_ref, _weighted_values(e.astype(BF16), v_ref[...]))


def _attn_ctx(qn, kt, vb):
    blk = S // TCX
    return pl.pallas_call(
        _attn_ctx_kernel,
        grid=(ATT_KVH,),
        in_specs=[pl.BlockSpec((TCX, 2 * ATT_HD), lambda h: (blk, h)),
                  pl.BlockSpec((ATT_HD, TCX), lambda h: (h, blk)),
                  pl.BlockSpec((TCX, 2 * ATT_HD), lambda h: (blk, h))],
        out_specs=pl.BlockSpec((TCX, 2 * ATT_HD), lambda h: (0, h)),
        out_shape=jax.ShapeDtypeStruct((TCX, 4 * ATT_HD), BF16),
        compiler_params=_cparams(("arbitrary",)),
        name="attn_ctx",
    )(qn, kt, vb)


def _softmax_rows(s_ref, p_ref, r):
    rows = slice(r * ATT_RB, (r + 1) * ATT_RB)
    ncb = NT // ATT_CB
    mx = s_ref[rows, 0:ATT_CB]
    for c in range(1, ncb):
        mx = jnp.maximum(mx, s_ref[rows, c * ATT_CB:(c + 1) * ATT_CB])
    m = jnp.max(mx, axis=-1, keepdims=True)
    for c in range(ncb):
        cols = slice(c * ATT_CB, (c + 1) * ATT_CB)
        p_ref[rows, cols] = jnp.exp2(s_ref[rows, cols] - m).astype(BF16)


def _attn_full_kernel(q_ref, kt_ref, v_ref, o0_ref, o1_ref, sa_ref, sb_ref, pa_ref, pb_ref):
    @pl.when(pl.program_id(0) == 0)
    def _():
        sb_ref[...] = jnp.zeros_like(sb_ref)

    def stacked_q(h):
        return jnp.concatenate([q_ref[:, 2 * h * ATT_HD:(2 * h + 1) * ATT_HD],
                                q_ref[:, (2 * h + 1) * ATT_HD:(2 * h + 2) * ATT_HD]], axis=0)

    nrb = 2 * ATT_TQ // ATT_RB
    sa_ref[...] = _dot(stacked_q(0), kt_ref[0:ATT_HD, :])
    for r in range(nrb):
        _softmax_rows(sb_ref, pb_ref, r)
    _store_heads(o1_ref, _weighted_values(pb_ref[...], v_ref[:, 2 * ATT_HD:4 * ATT_HD]))
    for r in range(nrb):
        _softmax_rows(sa_ref, pa_ref, r)
    sb_ref[...] = _dot(stacked_q(1), kt_ref[ATT_HD:2 * ATT_HD, :])
    _store_heads(o0_ref, _weighted_values(pa_ref[...], v_ref[:, 0:2 * ATT_HD]))


def _attn_latent(qn, kt, vb):
    nq = S // ATT_TQ
    cur = lambda g: jnp.minimum(g, nq - 1)
    prev = lambda g: jnp.maximum(g - 1, 0)
    resident = dict(pipeline_mode=pl.Buffered(1))
    half = jax.ShapeDtypeStruct((S, 2 * ATT_HD), BF16)
    return pl.pallas_call(
        _attn_full_kernel,
        grid=(nq + 1,),
        in_specs=[pl.BlockSpec((ATT_TQ, 4 * ATT_HD), lambda g: (cur(g), 0)),
                  pl.BlockSpec((ATT_KVH * ATT_HD, NT), lambda g: (0, 0), **resident),
                  pl.BlockSpec((NT, 2 * ATT_KVH * ATT_HD), lambda g: (0, 0), **resident)],
        out_specs=[pl.BlockSpec((ATT_TQ, 2 * ATT_HD), lambda g: (cur(g), 0)),
                   pl.BlockSpec((ATT_TQ, 2 * ATT_HD), lambda g: (prev(g), 0))],
        out_shape=[half, half],
        scratch_shapes=[pltpu.VMEM((2 * ATT_TQ, NT), F32),
                        pltpu.VMEM((2 * ATT_TQ, NT), F32),
                        pltpu.VMEM((2 * ATT_TQ, NT), BF16),
                        pltpu.VMEM((2 * ATT_TQ, NT), BF16)],
        compiler_params=_cparams(("arbitrary",)),
        name="attn_latent",
    )(qn, kt, vb)


def _attn_mixer(p, cos_t, sin_t, qk_g):
    qn, kt, vb = _attn_prep(p, cos_t, sin_t, qk_g)
    y0, y1 = _attn_latent(qn, kt, vb)
    return jnp.concatenate([jnp.concatenate([y0, y1], axis=1), _attn_ctx(qn, kt, vb)], axis=0)


def _walk_block(s, nlat, reverse):
    if reverse:
        return jnp.where(s == 0, nlat, nlat - s)
    return jnp.where(s == 0, nlat, s - 1)


def _seq_edges(s, nlat, reverse):
    if reverse:
        return (s == 0) | (s == nlat), (s == 0) | (s == 1)
    return (s == 0) | (s == 1), (s == 0) | (s == nlat)


def _lru_kernel(xb_ref, xp_ref, xn_ref, cw_ref, cb_ref, wg_ref, bg_ref, lam_ref, *rest,
                nlat, reverse):
    if reverse:
        hf_ref, g_ref, y_ref, xe_ref, carry_ref = rest
    else:
        h_ref, xe_ref, carry_ref = rest
    s = pl.program_id(0)
    starts, ends = _seq_edges(s, nlat, reverse)

    @pl.when(s == 0)
    def _():
        carry_ref[...] = jnp.zeros_like(carry_ref)

    xe_ref[0:8, :] = jnp.where(starts, 0.0, xp_ref[...])
    xe_ref[8:8 + TB, :] = xb_ref[...]
    xe_ref[8 + TB:16 + TB, :] = jnp.where(ends, 0.0, xn_ref[...])
    u = cb_ref[...]
    for j in range(4):
        u = u + cw_ref[j:j + 1, :] * xe_ref[6 + j:6 + j + TB, :]

    gates = _dot(u.astype(BF16), wg_ref[...]) + bg_ref[...]
    r = _sigmoid(gates[:, :DG])
    i = _sigmoid(gates[:, DG:])
    a = jnp.exp((-LRU_C) * r * _softplus(-lam_ref[...]))
    b = jnp.sqrt(1.0 - a * a) * (i * u)

    row8 = lax.broadcasted_iota(jnp.int32, (TB, 1), 0) % 8
    for sh in (1, 2, 4):
        if reverse:
            a_s, b_s, valid = pltpu.roll(a, TB - sh, 0), pltpu.roll(b, TB - sh, 0), row8 < 8 - sh
        else:
            a_s, b_s, valid = pltpu.roll(a, sh, 0), pltpu.roll(b, sh, 0), row8 >= sh
        b = b + jnp.where(valid, a * b_s, 0.0)
        a = jnp.where(valid, a * a_s, a)
    carry = carry_ref[0:1, :]
    groups = [None] * (TB // 8)
    for gi in (range(TB // 8 - 1, -1, -1) if reverse else range(TB // 8)):
        hg = b[8 * gi:8 * gi + 8, :] + a[8 * gi:8 * gi + 8, :] * carry
        groups[gi] = hg
        carry = hg[0:1, :] if reverse else hg[7:8, :]
    h = jnp.concatenate(groups, axis=0)
    carry_ref[0:1, :] = carry

    if reverse:
        g = g_ref[...]
        gelu = 0.5 * g * (1.0 + jnp.tanh(0.7978845608028654 * (g + 0.044715 * (g * g * g))))
        y_ref[...] = ((hf_ref[...] + h) * gelu).astype(BF16)
    else:
        h_ref[...] = h


def _lru_pass(p, conv_w, conv_b, wg, bg, lam, nlat, reverse, hf=None):
    blk = lambda s: _walk_block(s, nlat, reverse)
    nrow8 = (nlat + 1) * TB // 8
    in_specs = [pl.BlockSpec((TB, DG), lambda s: (blk(s), 0)),
                pl.BlockSpec((8, DG), lambda s: (jnp.maximum(blk(s) * (TB // 8) - 1, 0), 0)),
                pl.BlockSpec((8, DG), lambda s: (jnp.minimum((blk(s) + 1) * (TB // 8), nrow8 - 1), 0)),
                pl.BlockSpec((4, DG), lambda s: (0, 0)),
                pl.BlockSpec((1, DG), lambda s: (0, 0)),
                pl.BlockSpec((DG, 2 * DG), lambda s: (0, 0)),
                pl.BlockSpec((1, 2 * DG), lambda s: (0, 0)),
                pl.BlockSpec((1, DG), lambda s: (0, 0))]
    args = [p, p, p, conv_w, conv_b, wg, bg, lam]
    if reverse:
        in_specs += [pl.BlockSpec((TB, DG), lambda s: (blk(s), 0)),
                     pl.BlockSpec((TB, DG), lambda s: (blk(s), 1))]
        args += [hf, p]
    return pl.pallas_call(
        functools.partial(_lru_kernel, nlat=nlat, reverse=reverse),
        grid=(nlat + 1,),
        in_specs=in_specs,
        out_specs=pl.BlockSpec((TB, DG), lambda s: (blk(s), 0)),
        out_shape=jax.ShapeDtypeStruct(((nlat + 1) * TB, DG), BF16 if reverse else F32),
        scratch_shapes=[pltpu.VMEM((TB + 16, DG), F32), pltpu.VMEM((8, DG), F32)],
        compiler_params=_cparams(("arbitrary",)),
        name="lru_bwd" if reverse else "lru_fwd",
    )(*args)


def _lru_gate_dense(w):
    eye = jnp.eye(LRU_HEADS, dtype=w.dtype)
    dense = jnp.einsum('ghij,hk->ghikj', w, eye).reshape(2, DG, DG)
    return jnp.concatenate([dense[0], dense[1]], axis=1).astype(BF16)


def _lru_mixer(p, conv_w, conv_b, gate_w, gate_b, lam, nlat=S // TB):
    cb = conv_b[None, :]
    hf = _lru_pass(p, conv_w, cb, _lru_gate_dense(gate_w[0]), gate_b[0].reshape(1, 2 * DG),
                   lam[0:1], nlat, False)
    return _lru_pass(p, conv_w, cb, _lru_gate_dense(gate_w[1]), gate_b[1].reshape(1, 2 * DG),
                     lam[1:2], nlat, True, hf=hf)


def _split3(x):
    hi = x.astype(BF16)
    r1 = x - hi.astype(F32)
    mid = r1.astype(BF16)
    lo = (r1 - mid.astype(F32)).astype(BF16)
    return hi, mid, lo


def _gla_kernel(q_ref, k_ref, v_ref, lr_ref, wl_ref, bl_ref, *rest, reverse):
    if reverse:
        of_ref, g_ref, og_ref, y_ref, st_ref, o_acc = rest
    else:
        o_acc, st_ref = rest
    s = pl.program_id(0)
    nck = TB // GLA_CHUNK
    hw = GLA_HEADS * GLA_DK

    @pl.when(s == 0)
    def _():
        st_ref[...] = jnp.zeros_like(st_ref)

    x = _dot(lr_ref[...].astype(BF16), wl_ref[...]) + bl_ref[...]
    log_a = (jnp.minimum(x, 0.0) - jnp.log(1.0 + jnp.exp(-jnp.abs(x)))) * (1.0 / GLA_TAU)

    ri = lax.broadcasted_iota(jnp.int32, (TB, TB), 0)
    ci = lax.broadcasted_iota(jnp.int32, (TB, TB), 1)
    same = (ri // GLA_CHUNK) == (ci // GLA_CHUNK)
    causal = same & ((ci >= ri) if reverse else (ci <= ri))
    tri = jnp.where(causal, 1.0, 0.0).astype(BF16)
    ones = jnp.where(same, 1.0, 0.0).astype(BF16)
    parts = _split3(log_a)
    bcum = _dot(tri, parts[0]) + _dot(tri, parts[1]) + _dot(tri, parts[2])
    tot = _dot(ones, parts[0]) + _dot(ones, parts[1]) + _dot(ones, parts[2])

    q_t = q_ref[...] * (GLA_DK ** -0.5) * jnp.exp(bcum)
    k_in = (k_ref[...] * jnp.exp(-bcum)).astype(BF16)
    k_out = (k_ref[...] * jnp.exp(tot - bcum)).astype(BF16)
    decay = jnp.exp(tot)
    lane = lax.broadcasted_iota(jnp.int32, (1, hw), 1)

    st = st_ref[...]
    for h in range(GLA_HEADS):
        head = (lane // GLA_DK) == h
        qh = jnp.where(head, q_t, 0.0).astype(BF16)
        vh = v_ref[:, h * GLA_DV:(h + 1) * GLA_DV].astype(BF16)
        att = jnp.where(causal, _dot_nt(qh, k_in), 0.0)
        o_acc[:, h * GLA_DV:(h + 1) * GLA_DV] = _dot(att.astype(BF16), vh)
    for c in (range(nck - 1, -1, -1) if reverse else range(nck)):
        rows = slice(c * GLA_CHUNK, (c + 1) * GLA_CHUNK)
        new_st = []
        for h in range(GLA_HEADS):
            head = (lane // GLA_DK) == h
            st_h = st[h * GLA_DV:(h + 1) * GLA_DV, :]
            qh = jnp.where(head, q_t[rows, :], 0.0).astype(BF16)
            cols = slice(h * GLA_DV, (h + 1) * GLA_DV)
            o_acc[rows, cols] += _dot_nt(qh, st_h.astype(BF16))
            upd = _dot_tn(v_ref[rows, cols].astype(BF16), k_out[rows, :])
            new_st.append(decay[c * GLA_CHUNK:c * GLA_CHUNK + 1, :] * st_h + jnp.where(head, upd, 0.0))
        st = jnp.concatenate(new_st, axis=0)
    st_ref[...] = st

    if reverse:
        g = g_ref[...]
        og = og_ref[...]
        for h in range(GLA_HEADS):
            cols = slice(h * GLA_DV, (h + 1) * GLA_DV)
            o = of_ref[:, cols] + o_acc[:, cols]
            o = o * lax.rsqrt(jnp.mean(o * o, axis=-1, keepdims=True) + EPS) * og
            gh = g[:, cols]
            y_ref[:, cols] = (o * (gh * _sigmoid(gh))).astype(BF16)


def _gla_pass(p, lr, wl, bl, nlat, reverse, o_f=None, out_g=None):
    blk = lambda s: _walk_block(s, nlat, reverse)
    in_specs = [pl.BlockSpec((TB, 256), lambda s: (blk(s), 8)),
                pl.BlockSpec((TB, 256), lambda s: (blk(s), 9)),
                pl.BlockSpec((TB, DG), lambda s: (blk(s), 5)),
                pl.BlockSpec((TB, 128), lambda s: (blk(s), 0)),
                pl.BlockSpec((128, 256), lambda s: (0, 0)),
                pl.BlockSpec((1, 256), lambda s: (0, 0))]
    args = [p, p, p, lr, wl, bl]
    scratch = [pltpu.VMEM((GLA_HEADS * GLA_DV, GLA_HEADS * GLA_DK), F32)]
    if reverse:
        in_specs += [pl.BlockSpec((TB, DG), lambda s: (blk(s), 0)),
                     pl.BlockSpec((TB, DG), lambda s: (blk(s), 6)),
                     pl.BlockSpec((1, GLA_DV), lambda s: (0, 0))]
        args += [o_f, p, out_g]
        scratch = scratch + [pltpu.VMEM((TB, DG), F32)]
    return pl.pallas_call(
        functools.partial(_gla_kernel, reverse=reverse),
        grid=(nlat + 1,),
        in_specs=in_specs,
        out_specs=pl.BlockSpec((TB, DG), lambda s: (blk(s), 0)),
        out_shape=jax.ShapeDtypeStruct(((nlat + 1) * TB, DG), BF16 if reverse else F32),
        scratch_shapes=scratch,
        compiler_params=_cparams(("arbitrary",)),
        name="gla_bwd" if reverse else "gla_fwd",
    )(*args)


def _gla_gate_dense(gate_w, d):
    w = jnp.zeros((128, GLA_HEADS * GLA_DK), F32)
    return w.at[d * GLA_RANK:(d + 1) * GLA_RANK].set(gate_w[d]).astype(BF16)


def _gla_mixer(p, lr, gate_w, gate_b, out_g, nlat=S // TB):
    o_f = _gla_pass(p, lr, _gla_gate_dense(gate_w, 0), gate_b[0:1], nlat, False)
    return _gla_pass(p, lr, _gla_gate_dense(gate_w, 1), gate_b[1:2], nlat, True,
                     o_f=o_f, out_g=out_g[None, :])


FN1 = 64
FN2 = 128
FN2_TILE = 8
FK1_TILE = 8


def _dft_consts():
    def cs(n, rows=None, cols=None):
        a = 2.0 * np.pi * np.outer(np.arange(n if rows is None else rows),
                                   np.arange(n if cols is None else cols)) / n
        return np.cos(a), np.sin(a)

    c1, s1 = cs(FN1)
    m1 = np.concatenate([c1, -s1], axis=0)
    ang = 2.0 * np.pi * np.outer(np.arange(FN1), np.arange(FN2)) / (FN1 * FN2)
    tw_c = np.cos(ang).reshape(FN1, FN2 // FN2_TILE, FN2_TILE).transpose(1, 0, 2)
    tw_s = np.sin(ang).reshape(FN1, FN2 // FN2_TILE, FN2_TILE).transpose(1, 0, 2)
    c2, s2 = cs(FN2)
    m2 = np.block([[c2, s2], [-s2, c2]])
    cc, sc = cs(FNET_CH)
    eye = np.eye(DG // FNET_CH)
    ccb, scb = np.kron(eye, cc), np.kron(eye, sc)
    ch_lat = np.concatenate([ccb, scb], axis=0) / np.sqrt(FN1 * FN2 * FNET_CH)
    ch_ctx = np.concatenate([ccb, -scb], axis=1)
    ct, st = cs(TCX)
    t_ctx = np.concatenate([ct, st], axis=1) / np.sqrt(TCX * FNET_CH)
    bf = lambda a: jnp.asarray(a, F32).astype(BF16)
    return dict(m1=bf(m1), tw_c=jnp.asarray(tw_c, F32), tw_s=jnp.asarray(tw_s, F32),
                m2=bf(m2), ch_lat=bf(ch_lat), ch_ctx=bf(ch_ctx), t_ctx=bf(t_ctx))


def _fnet1_kernel(u_ref, m1_ref, tc_ref, ts_ref, z_ref):
    a = _dot(m1_ref[...], u_ref[...].astype(BF16))
    ar, ai = a[:FN1], a[FN1:]
    for n in range(FN2_TILE):
        sl = slice(n * DG, (n + 1) * DG)
        c = tc_ref[:, n:n + 1]
        sn = ts_ref[:, n:n + 1]
        z_ref[0, :, sl] = (ar[:, sl] * c + ai[:, sl] * sn).astype(BF16)
        z_ref[1, :, sl] = (ai[:, sl] * c - ar[:, sl] * sn).astype(BF16)


def _fnet2_kernel(z_ref, m2_ref, ch_ref, y_ref):
    for i in range(FK1_TILE):
        z = jnp.concatenate([z_ref[0, i], z_ref[1, i]], axis=0)
        y3 = _dot(m2_ref[...], z)
        lhs = jnp.concatenate([y3[:FN2], y3[FN2:]], axis=1).astype(BF16)
        y_ref[:, i * DG:(i + 1) * DG] = _dot(lhs, ch_ref[...]).astype(BF16)


def _fnet_ctx_kernel(u_ref, ch_ref, t_ref, y_ref):
    v =_dot(u_ref[...].astype(BF16), ch_ref[...])
    rhs = jnp.concatenate([v[:, :DG], v[:, DG:]], axis=0).astype(BF16)
    y_ref[...] = _dot(t_ref[...], rhs).astype(BF16)


def _fnet_mixer(u, consts):
    cols = FN2_TILE * DG
    z = pl.pallas_call(
        _fnet1_kernel,
        grid=(FN2 // FN2_TILE,),
        in_specs=[pl.BlockSpec((FN1, cols), lambda j: (0, j)),
                  pl.BlockSpec((2 * FN1, FN1), lambda j: (0, 0)),
                  pl.BlockSpec((None, FN1, FN2_TILE), lambda j: (j, 0, 0)),
                  pl.BlockSpec((None, FN1, FN2_TILE), lambda j: (j, 0, 0))],
        out_specs=pl.BlockSpec((2, FN1, cols), lambda j: (0, 0, j)),
        out_shape=jax.ShapeDtypeStruct((2, FN1, FN2 * DG), BF16),
        compiler_params=_cparams(("arbitrary",)),
        name="fnet_stage1",
    )(u.reshape(NT // FN2, FN2 * DG), consts['m1'], consts['tw_c'], consts['tw_s'])
    y = pl.pallas_call(
        _fnet2_kernel,
        grid=(FN1 // FK1_TILE,),
        in_specs=[pl.BlockSpec((2, FK1_TILE, FN2, DG), lambda j: (0, j, 0, 0)),
                  pl.BlockSpec((2 * FN2, 2 * FN2), lambda j: (0, 0)),
                  pl.BlockSpec((2 * DG, DG), lambda j: (0, 0))],
        out_specs=pl.BlockSpec((FN2, FK1_TILE * DG), lambda j: (0, j)),
        out_shape=jax.ShapeDtypeStruct((FN2, FN1 * DG), BF16),
        compiler_params=_cparams(("arbitrary",)),
        name="fnet_stage2",
    )(z.reshape(2, FN1, FN2, DG), consts['m2'], consts['ch_lat'])
    y_ctx = pl.pallas_call(
        _fnet_ctx_kernel,
        grid=(1,),
        in_specs=[pl.BlockSpec((TCX, DG), lambda j: (S // TCX, 0)),
                  pl.BlockSpec((DG, 2 * DG), lambda j: (0, 0)),
                  pl.BlockSpec((TCX, 2 * TCX), lambda j: (0, 0))],
        out_specs=pl.BlockSpec((TCX, DG), lambda j: (0, 0)),
        out_shape=jax.ShapeDtypeStruct((TCX, DG), BF16),
        compiler_params=_cparams(("arbitrary",)),
        name="fnet_ctx",
    )(u, consts['ch_ctx'], consts['t_ctx'])
    return jnp.concatenate([y.reshape(S, DG), y_ctx], axis=0)


def _rope_tables():
    t = np.arange(S)
    half = 32
    freqs = ROPE_THETA ** (-np.arange(half, dtype=np.float64) / half)
    ang_r = (t // GRID_W)[:, None] * freqs
    ang_c = (t % GRID_W)[:, None] * freqs
    cos = np.concatenate([np.cos(ang_r), np.cos(ang_r), np.cos(ang_c), np.cos(ang_c)], axis=1)
    sin = np.concatenate([-np.sin(ang_r), np.sin(ang_r), -np.sin(ang_c), np.sin(ang_c)], axis=1)
    cos = np.concatenate([cos, np.ones((TCX, ATT_HD))], axis=0)
    sin = np.concatenate([sin, np.zeros((TCX, ATT_HD))], axis=0)
    return jnp.asarray(cos, F32), jnp.asarray(sin, F32)


def _mod_rows(mods):
    m = mods[:, 0:2].reshape(DEPTH, 2, 3, 3, D).transpose(0, 2, 1, 3, 4).reshape(3 * DEPTH, 6, D)
    return jnp.pad(m, ((0, 0), (0, 2), (0, 0)))


def kernel(x, c, ctx, c_ctx, mod_w, mod_b, norm_g, final_g, ffn_w13, ffn_w2, w_in, w_out,
           lru_conv_w, lru_conv_b, lru_gate_w, lru_gate_b, lru_lambda, attn_qk_g,
           gla_gate_w, gla_gate_b, gla_out_g):
    xs = jnp.concatenate([x[0], ctx[0]], axis=0)
    s8 = jnp.concatenate([jax.nn.silu(c), jax.nn.silu(c_ctx)[None, :],
                          jnp.zeros((6, D), F32)], axis=0)
    m = _mod_rows(_mod_all(s8, mod_w, mod_b))
    g = norm_g.reshape(3 * DEPTH, 1, D)
    w_tail = jnp.concatenate([w_in[:, :, D_IN_MAIN + 2 * GLA_RANK:],
                              w_in[:, :, D_IN_MAIN:D_IN_MAIN + 2 * GLA_RANK],
                              jnp.zeros((DEPTH, D, W_TAIL - DG - 2 * GLA_RANK), F32)], axis=2)
    cos_t, sin_t = _rope_tables()
    consts = _dft_consts()

    for l in range(DEPTH):
        h = _ffn_a(xs, g, m, ffn_w13, l, 0)
        xs = _ffn_b(h, xs, m, ffn_w2, l, 0)
        p, u, lr = _proj_in(xs, g, m, w_in, w_tail, l)
        ya = _lru_mixer(p, lru_conv_w[l], lru_conv_b[l], lru_gate_w[l], lru_gate_b[l], lru_lambda[l])
        yb = _attn_mixer(p, cos_t, sin_t, attn_qk_g[l])
        yc = _gla_mixer(p, lr, gla_gate_w[l], gla_gate_b[l], gla_out_g[l])
        yd = _fnet_mixer(u, consts)
        xs = _proj_out(ya, yb, yc, yd, xs, m, w_out, l)
        h = _ffn_a(xs, g, m, ffn_w13, l, 1)
        xs = _ffn_b(h, xs, m, ffn_w2, l, 1)
    return _final_norm(xs, final_g[None, :])[None]
```
